```python
import math
import jax, jax.numpy as jnp
from jax import lax
import numpy as np

D_MODEL = 1024
BATCH = 4
SEQ = 8192
DEPTH = 1

GRID_W = 64
CTX_LEN = 256
N_ADA = 6
DIFF_HEADS = 8
DIFF_DH = 64
Q_W = 2 * DIFF_HEADS * DIFF_DH
K_W = 2 * DIFF_HEADS * DIFF_DH
V_W = DIFF_HEADS * 2 * DIFF_DH
ATTN_OUT = V_W
Q_BLK = 128
ROPE_BASE = 10000.0
SGU_CHUNK = 128
SGU_GROUPS = 8
SGU_W = 1024
SGU_GROUP_W = SGU_W // SGU_GROUPS
Z_W = 2 * SGU_W
GATE_W = 2 * D_MODEL
IN_W = Q_W + K_W + V_W + Z_W + GATE_W
PEER_HEADS = 8
PEER_N_KEYS = 128
PEER_N_EXPERTS = PEER_N_KEYS * PEER_N_KEYS
PEER_TOPK = 16
PEER_PK_DIM = 128
PEER_Q_DIM = 2 * PEER_PK_DIM
PEER_BLK = 128
RMS_EPS = 1e-6

kernel_name = 'hybrid_diffattn_sgu_peer_dit_layer'


def rmsnorm(x, g=None, eps=RMS_EPS):
    xf = x.astype(jnp.float32)
    y = xf * lax.rsqrt(jnp.mean(xf * xf, axis=-1, keepdims=True) + eps)
    if g is not None:
        y = y * g.astype(jnp.float32)
    return y.astype(x.dtype)


def layernorm(x, g, b, eps=1e-5):
    xf = x.astype(jnp.float32)
    mu = jnp.mean(xf, axis=-1, keepdims=True)
    xc = xf - mu
    y = xc * lax.rsqrt(jnp.mean(xc * xc, axis=-1, keepdims=True) + eps)
    return (y * g.astype(jnp.float32) + b.astype(jnp.float32)).astype(x.dtype)


def modulate(h, shift, scale):
    return h * (1.0 + scale) + shift


def _rotate(xp, pos):
    nfreq = xp.shape[-1] // 2
    inv = ROPE_BASE ** (-jnp.arange(nfreq, dtype=jnp.float32) / nfreq)
    ang = pos.astype(jnp.float32)[:, None] * inv[None, :]
    cos = jnp.cos(ang)[None, :, None, :]
    sin = jnp.sin(ang)[None, :, None, :]
    x1 = xp[..., :nfreq].astype(jnp.float32)
    x2 = xp[..., nfreq:].astype(jnp.float32)
    return jnp.concatenate([x1 * cos - x2 * sin, x1 * sin + x2 * cos], axis=-1)


def axial_rope(x, rows, cols):
    half = x.shape[-1] // 2
    return jnp.concatenate([_rotate(x[..., :half], rows), _rotate(x[..., half:], cols)], axis=-1).astype(x.dtype)


def diff_attention(q, k, v, lam):
    B, L = q.shape[0], q.shape[1]
    Lk = k.shape[1]
    nb = L // Q_BLK
    qb = q.reshape(B, nb, Q_BLK, 2 * DIFF_HEADS, DIFF_DH).transpose(1, 0, 2, 3, 4)

    def one(qblk):
        s = jnp.einsum('bqhd,bkhd->bhqk', qblk, k).astype(jnp.float32) * (DIFF_DH ** -0.5)
        a = jax.nn.softmax(s, axis=-1).reshape(B, DIFF_HEADS, 2, Q_BLK, Lk)
        a = a[:, :, 0] - lam * a[:, :, 1]
        return jnp.einsum('bhqk,bkhe->bqhe', a.astype(v.dtype), v)

    o = lax.map(one, qb)
    return o.transpose(1, 0, 2, 3, 4).reshape(B, L, DIFF_HEADS, 2 * DIFF_DH)


def spatial_gating(z, ln_g, ln_b, w_s, b_s):
    u, v = jnp.split(z, 2, axis=-1)
    v = layernorm(v, ln_g, ln_b)
    B, L = v.shape[0], v.shape[1]
    vb = v.reshape(B, L // SGU_CHUNK, SGU_CHUNK, SGU_GROUPS, SGU_GROUP_W)
    mixed = jnp.einsum('gpq,bnqgc->bnpgc', w_s, vb) + jnp.transpose(b_s)[None, None, :, :, None]
    return u * mixed.reshape(B, L, SGU_W)


def split_proj(p):
    B, L = p.shape[0], p.shape[1]
    q, k, v, z, gl = jnp.split(p, [Q_W, Q_W + K_W, Q_W + K_W + V_W, Q_W + K_W + V_W + Z_W], axis=-1)
    q = q.reshape(B, L, 2 * DIFF_HEADS, DIFF_DH)
    k = k.reshape(B, L, 2 * DIFF_HEADS, DIFF_DH)
    v = v.reshape(B, L, DIFF_HEADS, 2 * DIFF_DH)
    return q, k, v, z, gl


def mix_branches(q, k_all, v_all, z, gate_logits, lam, lam_init, subln_g,
                 sgu_ln_g, sgu_ln_b, sgu_w, sgu_b, w_branch_attn, w_branch_sgu, w_out):
    B, L = q.shape[0], q.shape[1]
    o = diff_attention(q, k_all, v_all, lam)
    o = rmsnorm(o, subln_g) * (1.0 - lam_init)
    y_attn = o.reshape(B, L, ATTN_OUT) @ w_branch_attn
    y_sgu = spatial_gating(jax.nn.gelu(z, approximate=False), sgu_ln_g, sgu_ln_b, sgu_w, sgu_b) @ w_branch_sgu
    g_attn, g_sgu = jnp.split(jax.nn.sigmoid(gate_logits), 2, axis=-1)
    return (g_attn * y_attn + g_sgu * y_sgu) @ w_out


def peer(h, w_query, sub_keys, u_tab, v_tab):
    B, L, D = h.shape
    T = B * L
    xt = h.reshape(T, D)
    q = rmsnorm((xt @ w_query).reshape(T, PEER_HEADS, 2, PEER_PK_DIM))
    s = jnp.einsum('thpd,pnd->thpn', q, sub_keys).astype(jnp.float32)
    s1, i1 = lax.top_k(s[:, :, 0], PEER_TOPK)
    s2, i2 = lax.top_k(s[:, :, 1], PEER_TOPK)
    n_cand = PEER_TOPK * PEER_TOPK
    cand = (s1[..., :, None] + s2[..., None, :]).reshape(T, PEER_HEADS, n_cand)
    cidx = (i1[..., :, None] * PEER_N_KEYS + i2[..., None, :]).reshape(T, PEER_HEADS, n_cand)
    best, pos = lax.top_k(cand, PEER_TOPK)
    eidx = jnp.take_along_axis(cidx, pos, axis=-1)
    gate = jax.nn.softmax(best, axis=-1)
    nb = T // PEER_BLK

    def one(args):
        xb, ib, gb = args
        act = jax.nn.gelu(jnp.einsum('td,thkd->thk', xb, u_tab[ib]), approximate=False) * gb.astype(xb.dtype)
        return jnp.einsum('thk,thkd->td', act, v_tab[ib])

    out = lax.map(one, (xt.reshape(nb, PEER_BLK, D),
                        eidx.reshape(nb, PEER_BLK, PEER_HEADS, PEER_TOPK),
                        gate.reshape(nb, PEER_BLK, PEER_HEADS, PEER_TOPK)))
    return out.reshape(B, L, D)


def setup_inputs(seed: int = 0) -> dict:
    key = jax.random.key(seed)
    ks = jax.random.split(key, 28)
    D = D_MODEL

    def nrm(k, shape, scale):
        return jax.random.normal(k, shape, jnp.float32) * scale

    return {
        'x': nrm(ks[0], (BATCH, SEQ, D), 1.0),
        'c': nrm(ks[1], (BATCH, D), 1.0),
        'ctx': nrm(ks[2], (BATCH, CTX_LEN, D), 1.0),
        'c_ctx': nrm(ks[3], (D,), 1.0),
        'w_ada': nrm(ks[4], (DEPTH, D, N_ADA * D), 0.2 * D ** -0.5),
        'b_ada': nrm(ks[5], (DEPTH, N_ADA * D), 0.02),
        'g_pre_mix': 1.0 + nrm(ks[6], (DEPTH, D), 0.02),
        'g_post_mix': 1.0 + nrm(ks[7], (DEPTH, D), 0.02),
        'g_pre_ffn': 1.0 + nrm(ks[8], (DEPTH, D), 0.02),
        'g_post_ffn': 1.0 + nrm(ks[9], (DEPTH, D), 0.02),
        'w_in': nrm(ks[10], (DEPTH, D, IN_W), D ** -0.5),
        'b_gate': nrm(ks[11], (DEPTH, GATE_W), 0.02),
        'lambda_q1': nrm(ks[12], (DEPTH, DIFF_DH), 0.1),
        'lambda_k1': nrm(ks[13], (DEPTH, DIFF_DH), 0.1),
        'lambda_q2': nrm(ks[14], (DEPTH, DIFF_DH), 0.1),
        'lambda_k2': nrm(ks[15], (DEPTH, DIFF_DH), 0.1),
        'subln_g': 1.0 + nrm(ks[16], (DEPTH, 2 * DIFF_DH), 0.02),
        'sgu_ln_g': 1.0 + nrm(ks[17], (DEPTH, SGU_W), 0.02),
        'sgu_ln_b': nrm(ks[18], (DEPTH, SGU_W), 0.02),
        'sgu_w': nrm(ks[19], (DEPTH, SGU_GROUPS, SGU_CHUNK, SGU_CHUNK), SGU_CHUNK ** -0.5),
        'sgu_b': 1.0 + nrm(ks[20], (DEPTH, SGU_GROUPS, SGU_CHUNK), 0.02),
        'w_branch_attn': nrm(ks[21], (DEPTH, ATTN_OUT, D), ATTN_OUT ** -0.5),
        'w_branch_sgu': nrm(ks[22], (DEPTH, SGU_W, D), SGU_W ** -0.5),
        'w_out': nrm(ks[23], (DEPTH, D, D), D ** -0.5),
        'peer_w_query': nrm(ks[24], (DEPTH, D, PEER_HEADS * PEER_Q_DIM), D ** -0.5),
        'peer_sub_keys': nrm(ks[25], (DEPTH, 2, PEER_N_KEYS, PEER_PK_DIM), PEER_PK_DIM ** -0.5),
        'peer_u': nrm(ks[26], (DEPTH, PEER_N_EXPERTS, D), D ** -0.5),
        'peer_v': nrm(ks[27], (DEPTH, PEER_N_EXPERTS, D), D ** -0.5),
    }


def reference(x, c, ctx, c_ctx, w_ada, b_ada, g_pre_mix, g_post_mix, g_pre_ffn, g_post_ffn,
              w_in, b_gate, lambda_q1, lambda_k1, lambda_q2, lambda_k2, subln_g,
              sgu_ln_g, sgu_ln_b, sgu_w, sgu_b, w_branch_attn, w_branch_sgu, w_out,
              peer_w_query, peer_sub_keys, peer_u, peer_v):
    B, L, _ = x.shape
    Lc = ctx.shape[1]
    n_rows = L // GRID_W
    rows = jnp.repeat(jnp.arange(n_rows, dtype=jnp.int32), GRID_W)
    cols = jnp.tile(jnp.arange(GRID_W, dtype=jnp.int32), n_rows)

    for l in range(DEPTH):
        last = l == DEPTH - 1
        lam_init = 0.8 - 0.6 * math.exp(-0.3 * l)
        lam = (jnp.exp(jnp.dot(lambda_q1[l].astype(jnp.float32), lambda_k1[l].astype(jnp.float32)))
               - jnp.exp(jnp.dot(lambda_q2[l].astype(jnp.float32), lambda_k2[l].astype(jnp.float32)))
               + lam_init)
        mod = jax.nn.silu(c) @ w_ada[l] + b_ada[l]
        sh1, sc1, gt1, sh2, sc2, gt2 = [m[:, None, :] for m in jnp.split(mod, N_ADA, axis=-1)]
        mod_c = jax.nn.silu(c_ctx) @ w_ada[l] + b_ada[l]
        sh1c, sc1c, gt1c, sh2c, sc2c, gt2c = jnp.split(mod_c, N_ADA, axis=-1)

        h = modulate(rmsnorm(x, g_pre_mix[l]), sh1, sc1)
        hc = modulate(rmsnorm(ctx, g_pre_mix[l]), sh1c, sc1c)
        q, k, v, z, gl = split_proj(h @ w_in[l])
        q = axial_rope(q, rows, cols)
        k = axial_rope(k, rows, cols)
        if last:
            kv_c = hc @ w_in[l][:, Q_W:Q_W + K_W + V_W]
            kc = kv_c[..., :K_W].reshape(B, Lc, 2 * DIFF_HEADS, DIFF_DH)
            vc = kv_c[..., K_W:].reshape(B, Lc, DIFF_HEADS, 2 * DIFF_DH)
        else:
            qc, kc, vc, zc, glc = split_proj(hc @ w_in[l])
        k_all = jnp.concatenate([kc, k], axis=1)
        v_all = jnp.concatenate([vc, v], axis=1)
        y = mix_branches(q, k_all, v_all, z, gl + b_gate[l], lam, lam_init, subln_g[l],
                         sgu_ln_g[l], sgu_ln_b[l], sgu_w[l], sgu_b[l],
                         w_branch_attn[l], w_branch_sgu[l], w_out[l])
        x_new = x + gt1 * rmsnorm(y, g_post_mix[l])

        if not last:
            yc = mix_branches(qc, kc, vc, zc, glc + b_gate[l], lam, lam_init, subln_g[l],
                              sgu_ln_g[l], sgu_ln_b[l], sgu_w[l], sgu_b[l],
                              w_branch_attn[l], w_branch_sgu[l], w_out[l])
            ctx_new = ctx + gt1c * rmsnorm(yc, g_post_mix[l])
            h2c = modulate(rmsnorm(ctx_new, g_pre_ffn[l]), sh2c, sc2c)
            ctx = ctx_new + gt2c * rmsnorm(peer(h2c, peer_w_query[l], peer_sub_keys[l], peer_u[l], peer_v[l]), g_post_ffn[l])

        h2 = modulate(rmsnorm(x_new, g_pre_ffn[l]), sh2, sc2)
        x = x_new + gt2 * rmsnorm(peer(h2, peer_w_query[l], peer_sub_keys[l], peer_u[l], peer_v[l]), g_post_ffn[l])
    return x
```

```python
import functools
import math

import jax
import jax.numpy as jnp
from jax import lax
from jax.experimental import pallas as pl
from jax.experimental.pallas import tpu as pltpu

F32 = jnp.float32
BF16 = jnp.bfloat16

D = 1024
N_ADA = 6
GRID_W = 64
HEADS = 8
DH = 64
HEAD_W = 2 * DH
ROPE_BASE = 10000.0
SGU_CHUNK = 128
SGU_GROUPS = 8
SGU_W = 1024
Q_W = K_W = V_W = 1024
Z_W = 2 * SGU_W
Z_OFF = Q_W + K_W + V_W
G_OFF = Z_OFF + Z_W
PEER_HEADS = 8
PEER_KEYS = 128
PEER_TOPK = 16
PEER_EXPERTS = PEER_KEYS * PEER_KEYS
RMS_EPS = 1e-6
LN_EPS = 1e-5
LOG2E = 1.4426950408889634
INV_SQRT2 = 0.7071067811865476
NEG_BIG = -1e30

VMEM_LIMIT = 56 * 1024 * 1024


def _gelu(x):
    return 0.5 * x * (1.0 + lax.erf(x * INV_SQRT2))


def _rms(x, eps=RMS_EPS):
    return x * lax.rsqrt(jnp.mean(x * x, axis=-1, keepdims=True) + eps)


def _ada_kernel(cc_ref, w_ref, b_ref, lq1_ref, lk1_ref, lq2_ref, lk2_ref, mod_ref, lam_ref, *, lam_init):
    a = cc_ref[...]
    s = (a * jax.nn.sigmoid(a)).astype(BF16)
    mod_ref[...] = jnp.dot(s, w_ref[...].astype(BF16), preferred_element_type=F32) + b_ref[...]
    d1 = jnp.sum(lq1_ref[...] * lk1_ref[...], axis=-1, keepdims=True)
    d2 = jnp.sum(lq2_ref[...] * lk2_ref[...], axis=-1, keepdims=True)
    lam = jnp.exp(d1) - jnp.exp(d2) + lam_init
    lam_ref[...] = jnp.broadcast_to(lam, lam_ref.shape)


def _ada(cc, w_ada, b_ada, lq1, lk1, lq2, lk2, lam_init):
    tn = 1536
    n = w_ada.shape[1]
    vec = pl.BlockSpec((1, DH), lambda j: (0, 0))
    return pl.pallas_call(
        functools.partial(_ada_kernel, lam_init=lam_init),
        grid=(n // tn,),
        in_specs=[pl.BlockSpec((8, D), lambda j: (0, 0)),
                  pl.BlockSpec((D, tn), lambda j: (0, j)),
                  pl.BlockSpec((1, tn), lambda j: (0, j)),
                  vec, vec, vec, vec],
        out_specs=[pl.BlockSpec((8, tn), lambda j: (0, j)),
                   pl.BlockSpec((8, 128), lambda j: (0, 0))],
        out_shape=[jax.ShapeDtypeStruct((8, n), F32), jax.ShapeDtypeStruct((8, 128), F32)],
        compiler_params=pltpu.CompilerParams(dimension_semantics=("arbitrary",), vmem_limit_bytes=VMEM_LIMIT),
        name="ada",
    )(cc, w_ada, b_ada, lq1, lk1, lq2, lk2)


def _prenorm(x, g_ref, mod_ref, row, col0):
    sh = mod_ref[pl.ds(row, 1), col0:col0 + D]
    sc = mod_ref[pl.ds(row, 1), col0 + D:col0 + 2 * D]
    return _rms(x) * g_ref[...] * (1.0 + sc) + sh


def _inproj_kernel(x_ref, mod_ref, g_ref, win_ref, bg_ref, cos_ref, sin_ref, lng_ref, lnb_ref,
                   sw_ref, sb_ref, wbs_ref, q_ref, k_ref, v_ref, t_ref, ga_ref, s_scr, *, tm):
    b = pl.program_id(0)
    h = _prenorm(x_ref[0], g_ref, mod_ref, b, 0).astype(BF16)

    lane = lax.broadcasted_iota(jnp.int32, (tm, 128), 1)
    first = (lane & 16) == 0
    cosv = cos_ref[...]
    sinv = sin_ref[...]

    def rope(t):
        swapped = jnp.where(first, pltpu.roll(t, 112, 1), pltpu.roll(t, 16, 1))
        return t * cosv + swapped * sinv

    q = jnp.dot(h, win_ref[:, 0:Q_W], preferred_element_type=F32)
    for j in range(Q_W // 128):
        q_ref[0, :, j * 128:(j + 1) * 128] = (rope(q[:, j * 128:(j + 1) * 128]) * (DH ** -0.5 * LOG2E)).astype(BF16)
    k = jnp.dot(h, win_ref[:, Q_W:Q_W + K_W], preferred_element_type=F32)
    for j in range(K_W // 128):
        k_ref[0, :, j * 128:(j + 1) * 128] = rope(k[:, j * 128:(j + 1) * 128]).astype(BF16)
    v_ref[0] = jnp.dot(h, win_ref[:, Q_W + K_W:Z_OFF], preferred_element_type=F32).astype(BF16)

    z = _gelu(jnp.dot(h, win_ref[:, Z_OFF:G_OFF], preferred_element_type=F32))
    u = z[:, :SGU_W]
    vv = z[:, SGU_W:]
    mu = jnp.mean(vv, axis=-1, keepdims=True)
    xc = vv - mu
    vln = (xc * lax.rsqrt(jnp.mean(xc * xc, axis=-1, keepdims=True) + LN_EPS) * lng_ref[...] + lnb_ref[...]).astype(BF16)
    nc = tm // SGU_CHUNK
    gw = SGU_W // SGU_GROUPS
    for g in range(SGU_GROUPS):
        rhs = jnp.concatenate([vln[n * SGU_CHUNK:(n + 1) * SGU_CHUNK, g * gw:(g + 1) * gw] for n in range(nc)], axis=1)
        mixed = jnp.dot(sw_ref[g], rhs, preferred_element_type=F32)
        for n in range(nc):
            rows = slice(n * SGU_CHUNK, (n + 1) * SGU_CHUNK)
            cols = slice(g * gw, (g + 1) * gw)
            s_scr[rows, cols] = (u[rows, cols] * (mixed[:, n * gw:(n + 1) * gw] + sb_ref[g])).astype(BF16)
    y_sgu = jnp.dot(s_scr[...], wbs_ref[...], preferred_element_type=F32)

    gate = jax.nn.sigmoid(jnp.dot(h, win_ref[:, G_OFF:G_OFF + 2 * D], preferred_element_type=F32) + bg_ref[...])
    ga_ref[0] = gate[:, :D].astype(BF16)
    t_ref[0] = (gate[:, D:] * y_sgu).astype(BF16)


def _inproj(x, mod, g_pre, win_b, b_gate, cos_t, sin_t, ln_g, ln_b, sw_b, sb_b, wbs_b, tm):
    B, L, _ = x.shape
    full = lambda shape: pl.BlockSpec(shape, lambda b, i: (0,) * len(shape))
    tile = pl.BlockSpec((1, tm, D), lambda b, i: (b, i, 0))
    out = jax.ShapeDtypeStruct((B, L, D), BF16)
    return pl.pallas_call(
        functools.partial(_inproj_kernel, tm=tm),
        grid=(B, L // tm),
        in_specs=[tile, full(mod.shape), full((1, D)), full(win_b.shape), full((1, 2 * D)),
                  pl.BlockSpec((tm, 128), lambda b, i: (i, 0)), pl.BlockSpec((tm, 128), lambda b, i: (i, 0)),
                  full((1, SGU_W)), full((1, SGU_W)), full(sw_b.shape), full(sb_b.shape), full(wbs_b.shape)],
        out_specs=[tile] * 5,
        out_shape=[out] * 5,
        scratch_shapes=[pltpu.VMEM((tm, SGU_W), BF16)],
        compiler_params=pltpu.CompilerParams(dimension_semantics=("arbitrary", "arbitrary"),
                                             vmem_limit_bytes=VMEM_LIMIT),
        name="inproj",
    )(x, mod, g_pre, win_b, b_gate, cos_t, sin_t, ln_g, ln_b, sw_b, sb_b, wbs_b)


def _ctxkv_kernel(ctx_ref, mod_ref, g_ref, wk_ref, wv_ref, k_ref, v_ref, *, ctx_row):
    h = _prenorm(ctx_ref[0], g_ref, mod_ref, ctx_row, 0).astype(BF16)
    k_ref[0] = jnp.dot(h, wk_ref[...], preferred_element_type=F32).astype(BF16)
    v_ref[0] = jnp.dot(h, wv_ref[...], preferred_element_type=F32).astype(BF16)


def _ctxkv(ctx, mod, g_pre, win_b, ctx_row):
    B, Lc, _ = ctx.shape
    tile = pl.BlockSpec((1, Lc, D), lambda b: (b, 0, 0))
    out = jax.ShapeDtypeStruct((B, Lc, D), BF16)
    return pl.pallas_call(
        functools.partial(_ctxkv_kernel, ctx_row=ctx_row),
        grid=(B,),
        in_specs=[tile, pl.BlockSpec(mod.shape, lambda b: (0, 0)), pl.BlockSpec((1, D), lambda b: (0, 0)),
                  pl.BlockSpec((D, K_W), lambda b: (0, Q_W // K_W)),
                  pl.BlockSpec((D, V_W), lambda b: (0, (Q_W + K_W) // V_W))],
        out_specs=[tile, tile],
        out_shape=[out, out],
        compiler_params=pltpu.CompilerParams(dimension_semantics=("arbitrary",), vmem_limit_bytes=VMEM_LIMIT),
        name="ctxkv",
    )(ctx, mod, g_pre, win_b, win_b)


def _attn_kernel(q_ref, k_ref, v_ref, lam_ref, sg_ref, o_ref, m_scr, acc_scr, *, tq, tk, nk, post_scale):
    q = q_ref[0]
    lane = lax.broadcasted_iota(jnp.int32, (tq, HEAD_W), 1)
    zero = jnp.zeros_like(q)
    qs = jnp.concatenate([jnp.where(lane < DH, q, zero), jnp.where(lane >= DH, q, zero)], axis=0)
    m_scr[...] = jnp.full(m_scr.shape, NEG_BIG, F32)
    acc_scr[...] = jnp.zeros(acc_scr.shape, F32)
    ones = jnp.ones((tk, HEAD_W), BF16)

    def body(j, carry):
        off = pl.multiple_of(j * tk, tk)
        kb = k_ref[0, pl.ds(off, tk), :]
        vb = v_ref[0, pl.ds(off, tk), :]
        s = lax.dot_general(qs, kb, (((1,), (1,)), ((), ())), preferred_element_type=F32)
        m_old = m_scr[:, 0:1]
        m_new = jnp.maximum(m_old, jnp.max(s, axis=1, keepdims=True))
        alpha = jnp.exp2(m_old - m_new)
        p = jnp.exp2(s - m_new).astype(BF16)
        pv = jnp.dot(p, jnp.concatenate([vb, ones], axis=1), preferred_element_type=F32)
        acc_scr[...] = alpha * acc_scr[...] + pv
        m_scr[...] = jnp.broadcast_to(m_new, m_scr.shape)
        return carry

    lax.fori_loop(0, nk, body, 0)
    acc = acc_scr[...]
    o1 = acc[:tq, :HEAD_W] / acc[:tq, HEAD_W:]
    o2 = acc[tq:, :HEAD_W] / acc[tq:, HEAD_W:]
    o = o1 - lam_ref[0:1, :] * o2
    o_ref[0] = (_rms(o) * sg_ref[...] * post_scale).astype(BF16)


def _kv_tile(lk):
    best = 128
    for t in range(128, 1537, 128):
        if lk % t == 0:
            best = t
    return best


def _attn(q, k_all, v_all, lam, subln_g, lam_init, tq):
    B, L, _ = q.shape
    Lk = k_all.shape[1]
    tk = _kv_tile(Lk)
    kv = pl.BlockSpec((1, Lk, HEAD_W), lambda b, h, i: (b, 0, h))
    return pl.pallas_call(
        functools.partial(_attn_kernel, tq=tq, tk=tk, nk=Lk // tk, post_scale=1.0 - lam_init),
        grid=(B, HEADS, L // tq),
        in_specs=[pl.BlockSpec((1, tq, HEAD_W), lambda b, h, i: (b, i, h)), kv, kv,
                  pl.BlockSpec((8, 128), lambda b, h, i: (0, 0)),
                  pl.BlockSpec((1, HEAD_W), lambda b, h, i: (0, 0))],
        out_specs=pl.BlockSpec((1, tq, HEAD_W), lambda b, h, i: (b, i, h)),
        out_shape=jax.ShapeDtypeStruct((B, L, D), BF16),
        scratch_shapes=[pltpu.VMEM((2 * tq, 128), F32), pltpu.VMEM((2 * tq, 2 * HEAD_W), F32)],
        compiler_params=pltpu.CompilerParams(dimension_semantics=("arbitrary",) * 3, vmem_limit_bytes=VMEM_LIMIT),
        name="attn",
    )(q, k_all, v_all, lam, subln_g)


def _mixout_kernel(o_ref, ga_ref, t_ref, x_ref, mod_ref, gpost_ref, gpre_ref, wba_ref, wout_ref, xn_ref, h2t_ref):
    b = pl.program_id(0)
    y_attn = jnp.dot(o_ref[0], wba_ref[...], preferred_element_type=F32)
    merged = (ga_ref[0].astype(F32) * y_attn + t_ref[0].astype(F32)).astype(BF16)
    y = jnp.dot(merged, wout_ref[...], preferred_element_type=F32)
    gt1 = mod_ref[pl.ds(b, 1), 2 * D:3 * D]
    x_new = x_ref[0] + gt1 * (_rms(y) * gpost_ref[...])
    xn_ref[0] = x_new
    h2 = _prenorm(x_new, gpre_ref, mod_ref, b, 3 * D)
    h2t_ref[0] = h2.T.astype(BF16)


def _mixout(o, ga, t, x, mod, g_post, g_pre_ffn, wba_b, wout_b, tm):
    B, L, _ = x.shape
    full = lambda shape: pl.BlockSpec(shape, lambda b, i: (0,) * len(shape))
    tile = pl.BlockSpec((1, tm, D), lambda b, i: (b, i, 0))
    return pl.pallas_call(
        _mixout_kernel,
        grid=(B, L // tm),
        in_specs=[tile, tile, tile, tile, full(mod.shape), full((1, D)), full((1, D)), full((D, D)), full((D, D))],
        out_specs=[tile, pl.BlockSpec((1, D, tm), lambda b, i: (b, 0, i))],
        out_shape=[jax.ShapeDtypeStruct((B, L, D), F32), jax.ShapeDtypeStruct((B, D, L), BF16)],
        compiler_params=pltpu.CompilerParams(dimension_semantics=("arbitrary", "arbitrary"),
                                             vmem_limit_bytes=VMEM_LIMIT),
        name="mixout",
    )(o, ga, t, x, mod, g_post, g_pre_ffn, wba_b, wout_b)


def _peer_score(h2t, wq_ref, keys_ref, s1_scr, s2_scr, e1_scr, e2_scr, tau_scr, top_scr, tt):
    qt = jnp.dot(wq_ref[...], h2t, preferred_element_type=F32)
    neg_inf = jnp.float32(-jnp.inf)
    for h in range(PEER_HEADS):
        halves = []
        for p in range(2):
            r0 = (2 * h + p) * PEER_KEYS
            qhp = qt[r0:r0 + PEER_KEYS, :]
            qn = qhp * lax.rsqrt(jnp.mean(qhp * qhp, axis=0, keepdims=True) + RMS_EPS)
            s = jnp.dot(keys_ref[p], qn.astype(BF16), preferred_element_type=F32)
            halves.append(s)
            work = s
            for r in range(PEER_TOPK):
                m = jnp.max(work, axis=0, keepdims=True)
                top_scr[p, r:r + 1, :] = m
                if r + 1 < PEER_TOPK:
                    work = jnp.where(work == m, neg_inf, work)
        s1, s2 = halves
        a1 = top_scr[0]
        a2 = top_scr[1]
        cands = [a1[0:1, :] + a2]
        for j in range(1, 8):
            cands.append(a1[j:j + 1, :] + a2[0:8, :])
        cands.append(a1[8:16, :] + a2[0:1, :])
        work = list(cands)
        tau = None
        for r in range(PEER_TOPK):
            m = functools.reduce(jnp.maximum, [jnp.max(c, axis=0, keepdims=True) for c in work])
            if r + 1 < PEER_TOPK:
                work = [jnp.where(c == m, neg_inf, c) for c in work]
            else:
                tau = m
        cmax = a1[0:1, :] + a2[0:1, :]
        zsum = functools.reduce(
            lambda a, c: a + c,
            [jnp.sum(jnp.where(c >= tau, jnp.exp(c - cmax), 0.0), axis=0, keepdims=True) for c in cands])
        s1_scr[h] = s1
        s2_scr[h] = s2
        e1_scr[h] = jnp.exp(s1 - a1[0:1, :])
        e2_scr[h] = jnp.exp(s2 - a2[0:1, :]) / zsum
        tau_scr[h] = jnp.broadcast_to(tau, (8, tt))


def _peer_kernel(h2t_ref, xn_ref, mod_ref, gpost_ref, wq_ref, keys_ref, u_ref, vt_ref, out_ref,
                 s1_scr, s2_scr, e1_scr, e2_scr, tau_scr, top_scr, acc_scr, w_scr, *, tt, ch):
    b = pl.program_id(0)
    c = pl.program_id(2)
    h2t = h2t_ref[0]

    @pl.when(c == 0)
    def _():
        _peer_score(h2t, wq_ref, keys_ref, s1_scr, s2_scr, e1_scr, e2_scr, tau_scr, top_scr, tt)
        acc_scr[...] = jnp.zeros(acc_scr.shape, F32)

    act = _gelu(jnp.dot(u_ref[...], h2t, preferred_element_type=F32))
    for j in range(ch // PEER_KEYS):
        i1 = c * (ch // PEER_KEYS) + j
        g = jnp.zeros((PEER_KEYS, tt), F32)
        for h in range(PEER_HEADS):
            pair = s1_scr[h, pl.ds(i1, 1), :] + s2_scr[h]
            wgt = e1_scr[h, pl.ds(i1, 1), :] * e2_scr[h]
            g = g + jnp.where(pair >= tau_scr[h, 0:1, :], wgt, 0.0)
        rows = slice(j * PEER_KEYS, (j + 1) * PEER_KEYS)
        w_scr[rows, :] = (act[rows, :] * g).astype(BF16)
    acc_scr[...] += jnp.dot(vt_ref[...], w_scr[...], preferred_element_type=F32)

    @pl.when(c == pl.num_programs(2) - 1)
    def _():
        y = acc_scr[...].T
        gt2 = mod_ref[pl.ds(b, 1), 5 * D:6 * D]
        out_ref[0] = xn_ref[0] + gt2 * (_rms(y) * gpost_ref[...])


def _peer(h2t, x_new, mod, g_post, wq_t, keys_b, u_b, vt_b, tt, ch):
    B, _, L = h2t.shape
    full = lambda shape: pl.BlockSpec(shape, lambda b, i, c: (0,) * len(shape))
    tile = pl.BlockSpec((1, tt, D), lambda b, i, c: (b, i, 0))
    per_head = pltpu.VMEM((PEER_HEADS, PEER_KEYS, tt), F32)
    return pl.pallas_call(
        functools.partial(_peer_kernel, tt=tt, ch=ch),
        grid=(B, L // tt, PEER_EXPERTS // ch),
        in_specs=[pl.BlockSpec((1, D, tt), lambda b, i, c: (b, 0, i)), tile, full(mod.shape), full((1, D)),
                  full(wq_t.shape), full(keys_b.shape),
                  pl.BlockSpec((ch, D), lambda b, i, c: (c, 0)),
                  pl.BlockSpec((D, ch), lambda b, i, c: (0, c))],
        out_specs=tile,
        out_shape=jax.ShapeDtypeStruct((B, L, D), F32),
        scratch_shapes=[per_head, per_head, per_head, per_head,
                        pltpu.VMEM((PEER_HEADS, 8, tt), F32), pltpu.VMEM((2, PEER_TOPK, tt), F32),
                        pltpu.VMEM((D, tt), F32), pltpu.VMEM((ch, tt), BF16)],
        compiler_params=pltpu.CompilerParams(dimension_semantics=("arbitrary",) * 3, vmem_limit_bytes=VMEM_LIMIT),
        name="peer",
    )(h2t, x_new, mod, g_post, wq_t, keys_b, u_b, vt_b)


def _rope_tables(L):
    nfreq = DH // 4
    inv = ROPE_BASE ** (-jnp.arange(nfreq, dtype=F32) / nfreq)
    t = jnp.arange(L, dtype=jnp.int32)
    rows = (t // GRID_W).astype(F32)[:, None] * inv[None, :]
    cols = (t % GRID_W).astype(F32)[:, None] * inv[None, :]
    cos64 = jnp.concatenate([jnp.cos(rows), jnp.cos(rows), jnp.cos(cols), jnp.cos(cols)], axis=1)
    sin64 = jnp.concatenate([-jnp.sin(rows), jnp.sin(rows), -jnp.sin(cols), jnp.sin(cols)], axis=1)
    return jnp.tile(cos64, (1, 2)), jnp.tile(sin64, (1, 2))


def kernel(x, c, ctx, c_ctx, w_ada, b_ada, g_pre_mix, g_post_mix, g_pre_ffn, g_post_ffn, w_in, b_gate, lambda_q1, lambda_k1, lambda_q2, lambda_k2, subln_g, sgu_ln_g, sgu_ln_b, sgu_w, sgu_b, w_branch_attn, w_branch_sgu, w_out, peer_w_query, peer_sub_keys, peer_u, peer_v):
    B, L, _ = x.shape
    depth = w_ada.shape[0]
    assert depth == 1 and B <= 7 and L % 512 == 0
    lam_init = 0.8 - 0.6 * math.exp(-0.3 * 0)
    tm = 512 if L % 512 == 0 else 256

    row = lambda a: a[0].reshape(1, -1)
    cc = jnp.concatenate([c, c_ctx[None, :], jnp.zeros((8 - B - 1, D), F32)], axis=0)
    mod, lam = _ada(cc, w_ada[0], row(b_ada), row(lambda_q1), row(lambda_k1), row(lambda_q2), row(lambda_k2), lam_init)

    win_b = w_in[0].astype(BF16)
    cos_t, sin_t = _rope_tables(L)
    sb_b = jnp.broadcast_to(sgu_b[0][:, :, None], (SGU_GROUPS, SGU_CHUNK, SGU_W // SGU_GROUPS))
    q, k, v, t_sgu, g_attn = _inproj(x, mod, row(g_pre_mix), win_b, row(b_gate), cos_t, sin_t,
                                     row(sgu_ln_g), row(sgu_ln_b), sgu_w[0].astype(BF16), sb_b,
                                     w_branch_sgu[0].astype(BF16), tm)
    kc, vc = _ctxkv(ctx, mod, row(g_pre_mix), win_b, B)
    k_all = jnp.concatenate([kc, k], axis=1)
    v_all = jnp.concatenate([vc, v], axis=1)
    o = _attn(q, k_all, v_all, lam, row(subln_g), lam_init, 256)

    x_new, h2t = _mixout(o, g_attn, t_sgu, x, mod, row(g_post_mix), row(g_pre_ffn),
                         w_branch_attn[0].astype(BF16), w_out[0].astype(BF16), tm)
    return _peer(h2t, x_new, mod, row(g_post_ffn), peer_w_query[0].T.astype(BF16),
                 peer_sub_keys[0].astype(BF16), peer_u[0].astype(BF16), peer_v[0].T.astype(BF16), 512, 512)
```

```python
import functools
import math

import jax
import jax.numpy as jnp
from jax import lax
from jax.experimental import pallas as pl
from jax.experimental.pallas import tpu as pltpu

F32 = jnp.float32
BF16 = jnp.bfloat16

D = 1024
N_ADA = 6
GRID_W = 64
HEADS = 8
DH = 64
HEAD_W = 2 * DH
ROPE_BASE = 10000.0
SGU_CHUNK = 128
SGU_GROUPS = 8
SGU_W = 1024
Q_W = K_W = V_W = 1024
Z_W = 2 * SGU_W
Z_OFF = Q_W + K_W + V_W
G_OFF = Z_OFF + Z_W
PEER_HEADS = 8
PEER_KEYS = 128
PEER_TOPK = 16
PEER_EXPERTS = PEER_KEYS * PEER_KEYS
RMS_EPS = 1e-6
LN_EPS = 1e-5
LOG2E = 1.4426950408889634
INV_SQRT2 = 0.7071067811865476
NEG_BIG = -1e30
LAM_W = 512

VMEM_LIMIT = 56 * 1024 * 1024


def _gelu(x):
    return 0.5 * x * (1.0 + lax.erf(x * INV_SQRT2))


def _rms(x, eps=RMS_EPS):
    return x * lax.rsqrt(jnp.mean(x * x, axis=-1, keepdims=True) + eps)


def _ada_kernel(cc_ref, w_ref, b_ref, lq1_ref, lk1_ref, lq2_ref, lk2_ref, mod_ref, lam_ref, *, lam_init):
    a = cc_ref[...]
    s = (a * jax.nn.sigmoid(a)).astype(BF16)
    mod_ref[...] = jnp.dot(s, w_ref[...].astype(BF16), preferred_element_type=F32) + b_ref[...]
    d1 = jnp.sum(lq1_ref[...] * lk1_ref[...], axis=-1, keepdims=True)
    d2 = jnp.sum(lq2_ref[...] * lk2_ref[...], axis=-1, keepdims=True)
    lam = jnp.exp(d1) - jnp.exp(d2) + lam_init
    lam_ref[...] = jnp.broadcast_to(lam, lam_ref.shape)


def _ada(cc, w_ada, b_ada, lq1, lk1, lq2, lk2, lam_init):
    tn = 1536
    n = w_ada.shape[1]
    vec = pl.BlockSpec((1, DH), lambda j: (0, 0))
    return pl.pallas_call(
        functools.partial(_ada_kernel, lam_init=lam_init),
        grid=(n // tn,),
        in_specs=[pl.BlockSpec((8, D), lambda j: (0, 0)),
                  pl.BlockSpec((D, tn), lambda j: (0, j)),
                  pl.BlockSpec((1, tn), lambda j: (0, j)),
                  vec, vec, vec, vec],
        out_specs=[pl.BlockSpec((8, tn), lambda j: (0, j)),
                   pl.BlockSpec((8, LAM_W), lambda j: (0, 0))],
        out_shape=[jax.ShapeDtypeStruct((8, n), F32), jax.ShapeDtypeStruct((8, LAM_W), F32)],
        compiler_params=pltpu.CompilerParams(dimension_semantics=("arbitrary",), vmem_limit_bytes=VMEM_LIMIT),
        name="ada",
    )(cc, w_ada, b_ada, lq1, lk1, lq2, lk2)


def _prenorm(x, g_ref, mod_ref, row, col0):
    sh = mod_ref[pl.ds(row, 1), col0:col0 + D]
    sc = mod_ref[pl.ds(row, 1), col0 + D:col0 + 2 * D]
    return _rms(x) * g_ref[...] * (1.0 + sc) + sh


def _inproj_kernel(x_ref, mod_ref, g_ref, win_ref, bg_ref, cos_ref, sin_ref, lng_ref, lnb_ref,
                   sw_ref, sb_ref, wbs_ref, q_ref, k_ref, vt_ref, t_ref, ga_ref, s_scr, *, tm):
    b = pl.program_id(0)
    h = _prenorm(x_ref[0], g_ref, mod_ref, b, 0).astype(BF16)

    lane = lax.broadcasted_iota(jnp.int32, (tm, 128), 1)
    first = (lane & 16) == 0
    cosv = cos_ref[...]
    sinv = sin_ref[...]

    def rope(t):
        swapped = jnp.where(first, pltpu.roll(t, 112, 1), pltpu.roll(t, 16, 1))
        return t * cosv + swapped * sinv

    q = jnp.dot(h, win_ref[:, 0:Q_W], preferred_element_type=F32)
    for j in range(Q_W // 128):
        q_ref[0, :, j * 128:(j + 1) * 128] = (rope(q[:, j * 128:(j + 1) * 128]) * (DH ** -0.5 * LOG2E)).astype(BF16)
    k = jnp.dot(h, win_ref[:, Q_W:Q_W + K_W], preferred_element_type=F32)
    for j in range(K_W // 128):
        k_ref[0, :, j * 128:(j + 1) * 128] = rope(k[:, j * 128:(j + 1) * 128]).astype(BF16)
    vt_ref[0] = jnp.dot(h, win_ref[:, Q_W + K_W:Z_OFF], preferred_element_type=F32).T.astype(BF16)

    z = _gelu(jnp.dot(h, win_ref[:, Z_OFF:G_OFF], preferred_element_type=F32))
    u = z[:, :SGU_W]
    vv = z[:, SGU_W:]
    mu = jnp.mean(vv, axis=-1, keepdims=True)
    xc = vv - mu
    vln = (xc * lax.rsqrt(jnp.mean(xc * xc, axis=-1, keepdims=True) + LN_EPS) * lng_ref[...] + lnb_ref[...]).astype(BF16)
    nc = tm // SGU_CHUNK
    gw = SGU_W // SGU_GROUPS
    for g in range(SGU_GROUPS):
        rhs = jnp.concatenate([vln[n * SGU_CHUNK:(n + 1) * SGU_CHUNK, g * gw:(g + 1) * gw] for n in range(nc)], axis=1)
        mixed = jnp.dot(sw_ref[g], rhs, preferred_element_type=F32)
        for n in range(nc):
            rows = slice(n * SGU_CHUNK, (n + 1) * SGU_CHUNK)
            cols = slice(g * gw, (g + 1) * gw)
            s_scr[rows, cols] = (u[rows, cols] * (mixed[:, n * gw:(n + 1) * gw] + sb_ref[g])).astype(BF16)
    y_sgu = jnp.dot(s_scr[...], wbs_ref[...], preferred_element_type=F32)

    gate = jax.nn.sigmoid(jnp.dot(h, win_ref[:, G_OFF:G_OFF + 2 * D], preferred_element_type=F32) + bg_ref[...])
    ga_ref[0] = gate[:, :D].astype(BF16)
    t_ref[0] = (gate[:, D:] * y_sgu).astype(BF16)


def _inproj(x, mod, g_pre, win_b, b_gate, cos_t, sin_t, ln_g, ln_b, sw_b, sb_b, wbs_b, tm):
    B, L, _ = x.shape
    full = lambda shape: pl.BlockSpec(shape, lambda b, i: (0,) * len(shape))
    tile = pl.BlockSpec((1, tm, D), lambda b, i: (b, i, 0))
    tile_t = pl.BlockSpec((1, D, tm), lambda b, i: (b, 0, i))
    out = jax.ShapeDtypeStruct((B, L, D), BF16)
    out_t = jax.ShapeDtypeStruct((B, D, L), BF16)
    return pl.pallas_call(
        functools.partial(_inproj_kernel, tm=tm),
        grid=(B, L // tm),
        in_specs=[tile, full(mod.shape), full((1, D)), full(win_b.shape), full((1, 2 * D)),
                  pl.BlockSpec((tm, 128), lambda b, i: (i, 0)), pl.BlockSpec((tm, 128), lambda b, i: (i, 0)),
                  full((1, SGU_W)), full((1, SGU_W)), full(sw_b.shape), full(sb_b.shape), full(wbs_b.shape)],
        out_specs=[tile, tile, tile_t, tile, tile],
        out_shape=[out, out, out_t, out, out],
        scratch_shapes=[pltpu.VMEM((tm, SGU_W), BF16)],
        compiler_params=pltpu.CompilerParams(dimension_semantics=("arbitrary", "arbitrary"),
                                             vmem_limit_bytes=VMEM_LIMIT),
        name="inproj",
    )(x, mod, g_pre, win_b, b_gate, cos_t, sin_t, ln_g, ln_b, sw_b, sb_b, wbs_b)


def _ctxkv_kernel(ctx_ref, mod_ref, g_ref, wk_ref, wv_ref, k_ref, vt_ref, *, ctx_row):
    h = _prenorm(ctx_ref[0], g_ref, mod_ref, ctx_row, 0).astype(BF16)
    k_ref[0] = jnp.dot(h, wk_ref[...], preferred_element_type=F32).astype(BF16)
    vt_ref[0] = jnp.dot(h, wv_ref[...], preferred_element_type=F32).T.astype(BF16)


def _ctxkv(ctx, mod, g_pre, win_b, ctx_row):
    B, Lc, _ = ctx.shape
    tile = pl.BlockSpec((1, Lc, D), lambda b: (b, 0, 0))
    return pl.pallas_call(
        functools.partial(_ctxkv_kernel, ctx_row=ctx_row),
        grid=(B,),
        in_specs=[tile, pl.BlockSpec(mod.shape, lambda b: (0, 0)), pl.BlockSpec((1, D), lambda b: (0, 0)),
                  pl.BlockSpec((D, K_W), lambda b: (0, Q_W // K_W)),
                  pl.BlockSpec((D, V_W), lambda b: (0, (Q_W + K_W) // V_W))],
        out_specs=[tile, pl.BlockSpec((1, D, Lc), lambda b: (b, 0, 0))],
        out_shape=[jax.ShapeDtypeStruct((B, Lc, D), BF16), jax.ShapeDtypeStruct((B, D, Lc), BF16)],
        compiler_params=pltpu.CompilerParams(dimension_semantics=("arbitrary",), vmem_limit_bytes=VMEM_LIMIT),
        name="ctxkv",
    )(ctx, mod, g_pre, win_b, win_b)


ONES_ROWS = 16


def _attn_kernel(q_ref, k_ref, vt_ref, lam_ref, sgt_ref, o_ref, s_scr, *, tq, tk, nk, post_scale):
    q = q_ref[0]
    lane = lax.broadcasted_iota(jnp.int32, (tq, HEAD_W), 1)
    zero = jnp.zeros_like(q)
    qs = jnp.concatenate([jnp.where(lane < DH, q, zero), jnp.where(lane >= DH, q, zero)], axis=0)
    ones = jnp.ones((ONES_ROWS, tk), BF16)

    def scores(j):
        kb = k_ref[0, j * tk:(j + 1) * tk, :]
        st = lax.dot_general(kb, qs, (((1,), (1,)), ((), ())), preferred_element_type=F32)
        s_scr[j % 2] = st
        return jnp.max(st, axis=0, keepdims=True)

    m = jnp.full((1, 2 * tq), NEG_BIG, F32)
    acc = jnp.zeros((HEAD_W + ONES_ROWS, 2 * tq), F32)
    blk_max = scores(0)
    for j in range(nk):
        nxt_max = scores(j + 1) if j + 1 < nk else None
        m_new = jnp.maximum(m, blk_max)
        alpha = jnp.exp2(m - m_new)
        pt = jnp.exp2(s_scr[j % 2] - m_new).astype(BF16)
        vte = jnp.concatenate([vt_ref[0, :, j * tk:(j + 1) * tk], ones], axis=0)
        acc = alpha * acc + jnp.dot(vte, pt, preferred_element_type=F32)
        m, blk_max = m_new, nxt_max

    o1 = acc[:HEAD_W, :tq] / acc[HEAD_W:HEAD_W + 1, :tq]
    o2 = acc[:HEAD_W, tq:] / acc[HEAD_W:HEAD_W + 1, tq:]
    ot = o1 - lam_ref[0:1, 0:tq] * o2
    on = ot * lax.rsqrt(jnp.mean(ot * ot, axis=0, keepdims=True) + RMS_EPS) * sgt_ref[...] * post_scale
    o_ref[0] = on.T.astype(BF16)


def _kv_tile(lk):
    best = 128
    for t in range(128, 1537, 128):
        if lk % t == 0:
            best = t
    return best


def _attn(q, k_all, vt_all, lam, sg_t, lam_init, tq):
    B, L, _ = q.shape
    Lk = k_all.shape[1]
    tk = _kv_tile(Lk)
    return pl.pallas_call(
        functools.partial(_attn_kernel, tq=tq, tk=tk, nk=Lk // tk, post_scale=1.0 - lam_init),
        grid=(B, HEADS, L // tq),
        in_specs=[pl.BlockSpec((1, tq, HEAD_W), lambda b, h, i: (b, i, h)),
                  pl.BlockSpec((1, Lk, HEAD_W), lambda b, h, i: (b, 0, h)),
                  pl.BlockSpec((1, HEAD_W, Lk), lambda b, h, i: (b, h, 0)),
                  pl.BlockSpec(lam.shape, lambda b, h, i: (0, 0)),
                  pl.BlockSpec((HEAD_W, tq), lambda b, h, i: (0, 0))],
        out_specs=pl.BlockSpec((1, tq, HEAD_W), lambda b, h, i: (b, i, h)),
        out_shape=jax.ShapeDtypeStruct((B, L, D), BF16),
        scratch_shapes=[pltpu.VMEM((2, tk, 2 * tq), F32)],
        compiler_params=pltpu.CompilerParams(dimension_semantics=("arbitrary",) * 3, vmem_limit_bytes=VMEM_LIMIT),
        name="attn",
    )(q, k_all, vt_all, lam, sg_t)


def _mixout_kernel(o_ref, ga_ref, t_ref, x_ref, mod_ref, gpost_ref, gpre_ref, wba_ref, wout_ref, xn_ref, h2t_ref):
    b = pl.program_id(0)
    y_attn = jnp.dot(o_ref[0], wba_ref[...], preferred_element_type=F32)
    merged = (ga_ref[0].astype(F32) * y_attn + t_ref[0].astype(F32)).astype(BF16)
    y = jnp.dot(merged, wout_ref[...], preferred_element_type=F32)
    gt1 = mod_ref[pl.ds(b, 1), 2 * D:3 * D]
    x_new = x_ref[0] + gt1 * (_rms(y) * gpost_ref[...])
    xn_ref[0] = x_new
    h2 = _prenorm(x_new, gpre_ref, mod_ref, b, 3 * D)
    h2t_ref[0] = h2.T.astype(BF16)


def _mixout(o, ga, t, x, mod, g_post, g_pre_ffn, wba_b, wout_b, tm):
    B, L, _ = x.shape
    full = lambda shape: pl.BlockSpec(shape, lambda b, i: (0,) * len(shape))
    tile = pl.BlockSpec((1, tm, D), lambda b, i: (b, i, 0))
    return pl.pallas_call(
        _mixout_kernel,
        grid=(B, L // tm),
        in_specs=[tile, tile, tile, tile, full(mod.shape), full((1, D)), full((1, D)), full((D, D)), full((D, D))],
        out_specs=[tile, pl.BlockSpec((1, D, tm), lambda b, i: (b, 0, i))],
        out_shape=[jax.ShapeDtypeStruct((B, L, D), F32), jax.ShapeDtypeStruct((B, D, L), BF16)],
        compiler_params=pltpu.CompilerParams(dimension_semantics=("arbitrary", "arbitrary"),
                                             vmem_limit_bytes=VMEM_LIMIT),
        name="mixout",
    )(o, ga, t, x, mod, g_post, g_pre_ffn, wba_b, wout_b)


def _peer_score(h2t, wq_ref, keys_ref, s1_scr, s2_scr, e1_scr, e2_scr, tau_scr, top_scr, tt):
    qt = jnp.dot(wq_ref[...], h2t, preferred_element_type=F32)
    neg_inf = jnp.float32(-jnp.inf)
    for h in range(PEER_HEADS):
        halves = []
        for p in range(2):
            r0 = (2 * h + p) * PEER_KEYS
            qhp = qt[r0:r0 + PEER_KEYS, :]
            qn = qhp * lax.rsqrt(jnp.mean(qhp * qhp, axis=0, keepdims=True) + RMS_EPS)
            s = jnp.dot(keys_ref[p], qn.astype(BF16), preferred_element_type=F32)
            halves.append(s)
            work = s
            for r in range(PEER_TOPK):
                m = jnp.max(work, axis=0, keepdims=True)
                top_scr[p, r:r + 1, :] = m
                if r + 1 < PEER_TOPK:
                    work = jnp.where(work == m, neg_inf, work)
        s1, s2 = halves
        a1 = top_scr[0]
        a2 = top_scr[1]
        cands = [a1[0:1, :] + a2]
        for j in range(1, 8):
            cands.append(a1[j:j + 1, :] + a2[0:8, :])
        cands.append(a1[8:16, :] + a2[0:1, :])
        work = list(cands)
        tau = None
        for r in range(PEER_TOPK):
            m = functools.reduce(jnp.maximum, [jnp.max(c, axis=0, keepdims=True) for c in work])
            if r + 1 < PEER_TOPK:
                work = [jnp.where(c == m, neg_inf, c) for c in work]
            else:
                tau = m
        cmax = a1[0:1, :] + a2[0:1, :]
        zsum = functools.reduce(
            lambda a, c: a + c,
            [jnp.sum(jnp.where(c >= tau, jnp.exp(c - cmax), 0.0), axis=0, keepdims=True) for c in cands])
        s1_scr[h] = s1
        s2_scr[h] = s2
        e1_scr[h] = jnp.exp(s1 - a1[0:1, :])
        e2_scr[h] = jnp.exp(s2 - a2[0:1, :]) / zsum
        tau_scr[h] = jnp.broadcast_to(tau, (8, tt))


def _peer_kernel(h2t_ref, xn_ref, mod_ref, gpost_ref, wq_ref, keys_ref, u_ref, vt_ref, out_ref,
                 s1_scr, s2_scr, e1_scr, e2_scr, tau_scr, top_scr, acc_scr, w_scr, *, tt, ch):
    b = pl.program_id(0)
    c = pl.program_id(2)
    h2t = h2t_ref[0]

    @pl.when(c == 0)
    def _():
        _peer_score(h2t, wq_ref, keys_ref, s1_scr, s2_scr, e1_scr, e2_scr, tau_scr, top_scr, tt)
        acc_scr[...] = jnp.zeros(acc_scr.shape, F32)

    act = _gelu(jnp.dot(u_ref[...], h2t, preferred_element_type=F32))
    for j in range(ch // PEER_KEYS):
        i1 = c * (ch // PEER_KEYS) + j
        g = jnp.zeros((PEER_KEYS, tt), F32)
        for h in range(PEER_HEADS):
            pair = s1_scr[h, pl.ds(i1, 1), :] + s2_scr[h]
            wgt = e1_scr[h, pl.ds(i1, 1), :] * e2_scr[h]
            g = g + jnp.where(pair >= tau_scr[h, 0:1, :], wgt, 0.0)
        rows = slice(j * PEER_KEYS, (j + 1) * PEER_KEYS)
        w_scr[rows, :] = (act[rows, :] * g).astype(BF16)
    acc_scr[...] += jnp.dot(vt_ref[...], w_scr[...], preferred_element_type=F32)

    @pl.when(c == pl.num_programs(2) - 1)
    def _():
        y = acc_scr[...].T
        gt2 = mod_ref[pl.ds(b, 1), 5 * D:6 * D]
        out_ref[0] = xn_ref[0] + gt2 * (_rms(y) * gpost_ref[...])


def _peer(h2t, x_new, mod, g_post, wq_t, keys_b, u_b, vt_b, tt, ch):
    B, _, L = h2t.shape
    full = lambda shape: pl.BlockSpec(shape, lambda b, i, c: (0,) * len(shape))
    tile = pl.BlockSpec((1, tt, D), lambda b, i, c: (b, i, 0))
    per_head = pltpu.VMEM((PEER_HEADS, PEER_KEYS, tt), F32)
    return pl.pallas_call(
        functools.partial(_peer_kernel, tt=tt, ch=ch),
        grid=(B, L // tt, PEER_EXPERTS // ch),
        in_specs=[pl.BlockSpec((1, D, tt), lambda b, i, c: (b, 0, i)), tile, full(mod.shape), full((1, D)),
                  full(wq_t.shape), full(keys_b.shape),
                  pl.BlockSpec((ch, D), lambda b, i, c: (c, 0)),
                  pl.BlockSpec((D, ch), lambda b, i, c: (0, c))],
        out_specs=tile,
        out_shape=jax.ShapeDtypeStruct((B, L, D), F32),
        scratch_shapes=[per_head, per_head, per_head, per_head,
                        pltpu.VMEM((PEER_HEADS, 8, tt), F32), pltpu.VMEM((2, PEER_TOPK, tt), F32),
                        pltpu.VMEM((D, tt), F32), pltpu.VMEM((ch, tt), BF16)],
        compiler_params=pltpu.CompilerParams(dimension_semantics=("arbitrary",) * 3, vmem_limit_bytes=VMEM_LIMIT),
        name="peer",
    )(h2t, x_new, mod, g_post, wq_t, keys_b, u_b, vt_b)


def _rope_tables(L):
    nfreq = DH // 4
    inv = ROPE_BASE ** (-jnp.arange(nfreq, dtype=F32) / nfreq)
    t = jnp.arange(L, dtype=jnp.int32)
    rows = (t // GRID_W).astype(F32)[:, None] * inv[None, :]
    cols = (t % GRID_W).astype(F32)[:, None] * inv[None, :]
    cos64 = jnp.concatenate([jnp.cos(rows), jnp.cos(rows), jnp.cos(cols), jnp.cos(cols)], axis=1)
    sin64 = jnp.concatenate([-jnp.sin(rows), jnp.sin(rows), -jnp.sin(cols), jnp.sin(cols)], axis=1)
    return jnp.tile(cos64, (1, 2)), jnp.tile(sin64, (1, 2))


def kernel(x, c, ctx, c_ctx, w_ada, b_ada, g_pre_mix, g_post_mix, g_pre_ffn, g_post_ffn, w_in, b_gate, lambda_q1, lambda_k1, lambda_q2, lambda_k2, subln_g, sgu_ln_g, sgu_ln_b, sgu_w, sgu_b, w_branch_attn, w_branch_sgu, w_out, peer_w_query, peer_sub_keys, peer_u, peer_v):
    B, L, _ = x.shape
    depth = w_ada.shape[0]
    assert depth == 1 and B <= 7 and L % 512 == 0
    lam_init = 0.8 - 0.6 * math.exp(-0.3 * 0)
    tm = 512 if L % 512 == 0 else 256

    row = lambda a: a[0].reshape(1, -1)
    cc = jnp.concatenate([c, c_ctx[None, :], jnp.zeros((8 - B - 1, D), F32)], axis=0)
    mod, lam = _ada(cc, w_ada[0], row(b_ada), row(lambda_q1), row(lambda_k1), row(lambda_q2), row(lambda_k2), lam_init)

    win_b = w_in[0].astype(BF16)
    cos_t, sin_t = _rope_tables(L)
    sb_b = jnp.broadcast_to(sgu_b[0][:, :, None], (SGU_GROUPS, SGU_CHUNK, SGU_W // SGU_GROUPS))
    q, k, vt, t_sgu, g_attn = _inproj(x, mod, row(g_pre_mix), win_b, row(b_gate), cos_t, sin_t,
                                      row(sgu_ln_g), row(sgu_ln_b), sgu_w[0].astype(BF16), sb_b,
                                      w_branch_sgu[0].astype(BF16), tm)
    kc, vct = _ctxkv(ctx, mod, row(g_pre_mix), win_b, B)
    k_all = jnp.concatenate([kc, k], axis=1)
    vt_all = jnp.concatenate([vct, vt], axis=2)
    tq = 256
    sg_t = jnp.broadcast_to(subln_g[0][:, None], (HEAD_W, tq))
    o = _attn(q, k_all, vt_all, lam, sg_t, lam_init, tq)

    x_new, h2t = _mixout(o, g_attn, t_sgu, x, mod, row(g_post_mix), row(g_pre_ffn),
                         w_branch_attn[0].astype(BF16), w_out[0].astype(BF16), tm)
    return _peer(h2t, x_new, mod, row(g_post_ffn), peer_w_query[0].T.astype(BF16),
                 peer_sub_keys[0].astype(BF16), peer_u[0].astype(BF16), peer_v[0].T.astype(BF16), 512, 512)
```

```python
import functools
import math

import jax
import jax.numpy as jnp
from jax import lax
from jax.experimental import pallas as pl
from jax.experimental.pallas import tpu as pltpu

F32 = jnp.float32
BF16 = jnp.bfloat16

D = 1024
N_ADA = 6
GRID_W = 64
HEADS = 8
DH = 64
HEAD_W = 2 * DH
ROPE_BASE = 10000.0
SGU_CHUNK = 128
SGU_GROUPS = 8
SGU_W = 1024
Q_W = K_W = V_W = 1024
Z_W = 2 * SGU_W
Z_OFF = Q_W + K_W + V_W
G_OFF = Z_OFF + Z_W
PEER_HEADS = 8
PEER_KEYS = 128
PEER_TOPK = 16
PEER_EXPERTS = PEER_KEYS * PEER_KEYS
RMS_EPS = 1e-6
LN_EPS = 1e-5
LOG2E = 1.4426950408889634
INV_SQRT2 = 0.7071067811865476
NEG_BIG = -1e30
LAM_W = 512

VMEM_LIMIT = 56 * 1024 * 1024


def _gelu(x):
    return 0.5 * x * (1.0 + lax.erf(x * INV_SQRT2))


def _rms(x, eps=RMS_EPS):
    return x * lax.rsqrt(jnp.mean(x * x, axis=-1, keepdims=True) + eps)


def _ada_kernel(cc_ref, w_ref, b_ref, lq1_ref, lk1_ref, lq2_ref, lk2_ref, mod_ref, lam_ref, *, lam_init):
    a = cc_ref[...]
    s = (a * jax.nn.sigmoid(a)).astype(BF16)
    mod_ref[...] = jnp.dot(s, w_ref[...].astype(BF16), preferred_element_type=F32) + b_ref[...]
    d1 = jnp.sum(lq1_ref[...] * lk1_ref[...], axis=-1, keepdims=True)
    d2 = jnp.sum(lq2_ref[...] * lk2_ref[...], axis=-1, keepdims=True)
    lam = jnp.exp(d1) - jnp.exp(d2) + lam_init
    lam_ref[...] = jnp.broadcast_to(lam, lam_ref.shape)


def _ada(cc, w_ada, b_ada, lq1, lk1, lq2, lk2, lam_init):
    tn = 1536
    n = w_ada.shape[1]
    vec = pl.BlockSpec((1, DH), lambda j: (0, 0))
    return pl.pallas_call(
        functools.partial(_ada_kernel, lam_init=lam_init),
        grid=(n // tn,),
        in_specs=[pl.BlockSpec((8, D), lambda j: (0, 0)),
                  pl.BlockSpec((D, tn), lambda j: (0, j)),
                  pl.BlockSpec((1, tn), lambda j: (0, j)),
                  vec, vec, vec, vec],
        out_specs=[pl.BlockSpec((8, tn), lambda j: (0, j)),
                   pl.BlockSpec((8, LAM_W), lambda j: (0, 0))],
        out_shape=[jax.ShapeDtypeStruct((8, n), F32), jax.ShapeDtypeStruct((8, LAM_W), F32)],
        compiler_params=pltpu.CompilerParams(dimension_semantics=("arbitrary",), vmem_limit_bytes=VMEM_LIMIT),
        name="ada",
    )(cc, w_ada, b_ada, lq1, lk1, lq2, lk2)


def _prenorm(x, g_ref, mod_ref, row, col0):
    sh = mod_ref[pl.ds(row, 1), col0:col0 + D]
    sc = mod_ref[pl.ds(row, 1), col0 + D:col0 + 2 * D]
    return _rms(x) * g_ref[...] * (1.0 + sc) + sh


def _inproj_kernel(x_ref, mod_ref, g_ref, win_ref, bg_ref, cos_ref, sin_ref, lng_ref, lnb_ref,
                   sw_ref, sb_ref, wbs_ref, q_ref, k_ref, vt_ref, t_ref, ga_ref, s_scr, *, tm):
    b = pl.program_id(0)
    h = _prenorm(x_ref[0], g_ref, mod_ref, b, 0).astype(BF16)

    lane = lax.broadcasted_iota(jnp.int32, (tm, 128), 1)
    first = (lane & 16) == 0
    cosv = cos_ref[...]
    sinv = sin_ref[...]

    def rope(t):
        swapped = jnp.where(first, pltpu.roll(t, 112, 1), pltpu.roll(t, 16, 1))
        return t * cosv + swapped * sinv

    q = jnp.dot(h, win_ref[:, 0:Q_W], preferred_element_type=F32)
    for j in range(Q_W // 128):
        q_ref[0, :, j * 128:(j + 1) * 128] = (rope(q[:, j * 128:(j + 1) * 128]) * (DH ** -0.5 * LOG2E)).astype(BF16)
    k = jnp.dot(h, win_ref[:, Q_W:Q_W + K_W], preferred_element_type=F32)
    for j in range(K_W // 128):
        k_ref[0, :, j * 128:(j + 1) * 128] = rope(k[:, j * 128:(j + 1) * 128]).astype(BF16)
    vt_ref[0] = jnp.dot(h, win_ref[:, Q_W + K_W:Z_OFF], preferred_element_type=F32).T.astype(BF16)

    z = _gelu(jnp.dot(h, win_ref[:, Z_OFF:G_OFF], preferred_element_type=F32))
    u = z[:, :SGU_W]
    vv = z[:, SGU_W:]
    mu = jnp.mean(vv, axis=-1, keepdims=True)
    xc = vv - mu
    vln = (xc * lax.rsqrt(jnp.mean(xc * xc, axis=-1, keepdims=True) + LN_EPS) * lng_ref[...] + lnb_ref[...]).astype(BF16)
    nc = tm // SGU_CHUNK
    gw = SGU_W // SGU_GROUPS
    for g in range(SGU_GROUPS):
        rhs = jnp.concatenate([vln[n * SGU_CHUNK:(n + 1) * SGU_CHUNK, g * gw:(g + 1) * gw] for n in range(nc)], axis=1)
        mixed = jnp.dot(sw_ref[g], rhs, preferred_element_type=F32)
        for n in range(nc):
            rows = slice(n * SGU_CHUNK, (n + 1) * SGU_CHUNK)
            cols = slice(g * gw, (g + 1) * gw)
            s_scr[rows, cols] = (u[rows, cols] * (mixed[:, n * gw:(n + 1) * gw] + sb_ref[g])).astype(BF16)
    y_sgu = jnp.dot(s_scr[...], wbs_ref[...], preferred_element_type=F32)

    gate = jax.nn.sigmoid(jnp.dot(h, win_ref[:, G_OFF:G_OFF + 2 * D], preferred_element_type=F32) + bg_ref[...])
    ga_ref[0] = gate[:, :D].astype(BF16)
    t_ref[0] = (gate[:, D:] * y_sgu).astype(BF16)


def _inproj(x, mod, g_pre, win_b, b_gate, cos_t, sin_t, ln_g, ln_b, sw_b, sb_b, wbs_b, tm):
    B, L, _ = x.shape
    full = lambda shape: pl.BlockSpec(shape, lambda b, i: (0,) * len(shape))
    tile = pl.BlockSpec((1, tm, D), lambda b, i: (b, i, 0))
    tile_t = pl.BlockSpec((1, D, tm), lambda b, i: (b, 0, i))
    out = jax.ShapeDtypeStruct((B, L, D), BF16)
    out_t = jax.ShapeDtypeStruct((B, D, L), BF16)
    return pl.pallas_call(
        functools.partial(_inproj_kernel, tm=tm),
        grid=(B, L // tm),
        in_specs=[tile, full(mod.shape), full((1, D)), full(win_b.shape), full((1, 2 * D)),
                  pl.BlockSpec((tm, 128), lambda b, i: (i, 0)), pl.BlockSpec((tm, 128), lambda b, i: (i, 0)),
                  full((1, SGU_W)), full((1, SGU_W)), full(sw_b.shape), full(sb_b.shape), full(wbs_b.shape)],
        out_specs=[tile, tile, tile_t, tile, tile],
        out_shape=[out, out, out_t, out, out],
        scratch_shapes=[pltpu.VMEM((tm, SGU_W), BF16)],
        compiler_params=pltpu.CompilerParams(dimension_semantics=("arbitrary", "arbitrary"),
                                             vmem_limit_bytes=VMEM_LIMIT),
        name="inproj",
    )(x, mod, g_pre, win_b, b_gate, cos_t, sin_t, ln_g, ln_b, sw_b, sb_b, wbs_b)


def _ctxkv_kernel(ctx_ref, mod_ref, g_ref, wk_ref, wv_ref, k_ref, vt_ref, *, ctx_row):
    h = _prenorm(ctx_ref[0], g_ref, mod_ref, ctx_row, 0).astype(BF16)
    k_ref[0] = jnp.dot(h, wk_ref[...], preferred_element_type=F32).astype(BF16)
    vt_ref[0] = jnp.dot(h, wv_ref[...], preferred_element_type=F32).T.astype(BF16)


def _ctxkv(ctx, mod, g_pre, win_b, ctx_row):
    B, Lc, _ = ctx.shape
    tile = pl.BlockSpec((1, Lc, D), lambda b: (b, 0, 0))
    return pl.pallas_call(
        functools.partial(_ctxkv_kernel, ctx_row=ctx_row),
        grid=(B,),
        in_specs=[tile, pl.BlockSpec(mod.shape, lambda b: (0, 0)), pl.BlockSpec((1, D), lambda b: (0, 0)),
                  pl.BlockSpec((D, K_W), lambda b: (0, Q_W // K_W)),
                  pl.BlockSpec((D, V_W), lambda b: (0, (Q_W + K_W) // V_W))],
        out_specs=[tile, pl.BlockSpec((1, D, Lc), lambda b: (b, 0, 0))],
        out_shape=[jax.ShapeDtypeStruct((B, Lc, D), BF16), jax.ShapeDtypeStruct((B, D, Lc), BF16)],
        compiler_params=pltpu.CompilerParams(dimension_semantics=("arbitrary",), vmem_limit_bytes=VMEM_LIMIT),
        name="ctxkv",
    )(ctx, mod, g_pre, win_b, win_b)


ONES_ROWS = 16


def _attn_kernel(q_ref, k_ref, vt_ref, lam_ref, sgt_ref, o_ref, s_scr, *, tq, tk, nk, post_scale):
    q = q_ref[0]
    lane = lax.broadcasted_iota(jnp.int32, (tq, HEAD_W), 1)
    zero = jnp.zeros_like(q)
    qs = jnp.concatenate([jnp.where(lane < DH, q, zero), jnp.where(lane >= DH, q, zero)], axis=0)
    ones = jnp.ones((ONES_ROWS, tk), BF16)

    def scores(j):
        kb = k_ref[0, j * tk:(j + 1) * tk, :]
        st = lax.dot_general(kb, qs, (((1,), (1,)), ((), ())), preferred_element_type=F32)
        s_scr[j % 2] = st
        return jnp.max(st, axis=0, keepdims=True)

    m = jnp.full((1, 2 * tq), NEG_BIG, F32)
    acc = jnp.zeros((HEAD_W + ONES_ROWS, 2 * tq), F32)
    blk_max = scores(0)
    for j in range(nk):
        nxt_max = scores(j + 1) if j + 1 < nk else None
        m_new = jnp.maximum(m, blk_max)
        alpha = jnp.exp2(m - m_new)
        pt = jnp.exp2(s_scr[j % 2] - m_new).astype(BF16)
        vte = jnp.concatenate([vt_ref[0, :, j * tk:(j + 1) * tk], ones], axis=0)
        acc = alpha * acc + jnp.dot(vte, pt, preferred_element_type=F32)
        m, blk_max = m_new, nxt_max

    o1 = acc[:HEAD_W, :tq] / acc[HEAD_W:HEAD_W + 1, :tq]
    o2 = acc[:HEAD_W, tq:] / acc[HEAD_W:HEAD_W + 1, tq:]
    ot = o1 - lam_ref[0:1, 0:tq] * o2
    on = ot * lax.rsqrt(jnp.mean(ot * ot, axis=0, keepdims=True) + RMS_EPS) * sgt_ref[...] * post_scale
    o_ref[0] = on.T.astype(BF16)


def _kv_tile(lk):
    best = 128
    for t in range(128, 1537, 128):
        if lk % t == 0:
            best = t
    return best


def _attn(q, k_all, vt_all, lam, sg_t, lam_init, tq):
    B, L, _ = q.shape
    Lk = k_all.shape[1]
    tk = _kv_tile(Lk)
    return pl.pallas_call(
        functools.partial(_attn_kernel, tq=tq, tk=tk, nk=Lk // tk, post_scale=1.0 - lam_init),
        grid=(B, HEADS, L // tq),
        in_specs=[pl.BlockSpec((1, tq, HEAD_W), lambda b, h, i: (b, i, h)),
                  pl.BlockSpec((1, Lk, HEAD_W), lambda b, h, i: (b, 0, h)),
                  pl.BlockSpec((1, HEAD_W, Lk), lambda b, h, i: (b, h, 0)),
                  pl.BlockSpec(lam.shape, lambda b, h, i: (0, 0)),
                  pl.BlockSpec((HEAD_W, tq), lambda b, h, i: (0, 0))],
        out_specs=pl.BlockSpec((1, tq, HEAD_W), lambda b, h, i: (b, i, h)),
        out_shape=jax.ShapeDtypeStruct((B, L, D), BF16),
        scratch_shapes=[pltpu.VMEM((2, tk, 2 * tq), F32)],
        compiler_params=pltpu.CompilerParams(dimension_semantics=("arbitrary",) * 3, vmem_limit_bytes=VMEM_LIMIT),
        name="attn",
    )(q, k_all, vt_all, lam, sg_t)


def _mixout_kernel(o_ref, ga_ref, t_ref, x_ref, mod_ref, gpost_ref, gpre_ref, wba_ref, wout_ref, xn_ref, h2t_ref):
    b = pl.program_id(0)
    y_attn = jnp.dot(o_ref[0], wba_ref[...], preferred_element_type=F32)
    merged = (ga_ref[0].astype(F32) * y_attn + t_ref[0].astype(F32)).astype(BF16)
    y = jnp.dot(merged, wout_ref[...], preferred_element_type=F32)
    gt1 = mod_ref[pl.ds(b, 1), 2 * D:3 * D]
    x_new = x_ref[0] + gt1 * (_rms(y) * gpost_ref[...])
    xn_ref[0] = x_new
    h2 = _prenorm(x_new, gpre_ref, mod_ref, b, 3 * D)
    h2t_ref[0] = h2.T.astype(BF16)


def _mixout(o, ga, t, x, mod, g_post, g_pre_ffn, wba_b, wout_b, tm):
    B, L, _ = x.shape
    full = lambda shape: pl.BlockSpec(shape, lambda b, i: (0,) * len(shape))
    tile = pl.BlockSpec((1, tm, D), lambda b, i: (b, i, 0))
    return pl.pallas_call(
        _mixout_kernel,
        grid=(B, L // tm),
        in_specs=[tile, tile, tile, tile, full(mod.shape), full((1, D)), full((1, D)), full((D, D)), full((D, D))],
        out_specs=[tile, pl.BlockSpec((1, D, tm), lambda b, i: (b, 0, i))],
        out_shape=[jax.ShapeDtypeStruct((B, L, D), F32), jax.ShapeDtypeStruct((B, D, L), BF16)],
        compiler_params=pltpu.CompilerParams(dimension_semantics=("arbitrary", "arbitrary"),
                                             vmem_limit_bytes=VMEM_LIMIT),
        name="mixout",
    )(o, ga, t, x, mod, g_post, g_pre_ffn, wba_b, wout_b)


PEER_SUB = 4 * PEER_KEYS
PEER_TILE = (32, 128)


def _peer_score(h2t, wq_ref, keys_ref, th_scr, t2_scr, r1_scr, top_scr, tt):
    qt = jnp.dot(wq_ref[...], h2t, preferred_element_type=F32)
    neg_inf = jnp.float32(-jnp.inf)
    for h in range(PEER_HEADS):
        halves = []
        for p in range(2):
            r0 = (2 * h + p) * PEER_KEYS
            qhp = qt[r0:r0 + PEER_KEYS, :]
            qn = qhp * lax.rsqrt(jnp.mean(qhp * qhp, axis=0, keepdims=True) + RMS_EPS)
            s = jnp.dot(keys_ref[p], qn.astype(BF16), preferred_element_type=F32)
            halves.append(s)
            work = s
            for r in range(PEER_TOPK):
                m = jnp.max(work, axis=0, keepdims=True)
                top_scr[p, r:r + 1, :] = m
                if r + 1 < PEER_TOPK:
                    work = jnp.where(work == m, neg_inf, work)
        s1, s2 = halves
        a1 = top_scr[0]
        a2 = top_scr[1]
        cands = [a1[0:1, :] + a2]
        for j in range(1, 8):
            cands.append(a1[j:j + 1, :] + a2[0:8, :])
        cands.append(a1[8:16, :] + a2[0:1, :])
        work = list(cands)
        tau = None
        for r in range(PEER_TOPK):
            m = functools.reduce(jnp.maximum, [jnp.max(c, axis=0, keepdims=True) for c in work])
            if r + 1 < PEER_TOPK:
                work = [jnp.where(c == m, neg_inf, c) for c in work]
            else:
                tau = m
        cmax = a1[0:1, :] + a2[0:1, :]
        zsum = functools.reduce(
            lambda a, c: a + c,
            [jnp.sum(jnp.where(c >= tau, jnp.exp(c - cmax), 0.0), axis=0, keepdims=True) for c in cands])
        theta = jnp.full((PEER_KEYS, tt), jnp.inf, F32)
        for k in range(PEER_TOPK):
            a2k = a2[k:k + 1, :]
            theta = jnp.where(s1 + a2k >= tau, a2k, theta)
        lz = jnp.log2(zsum)
        to_log2 = lambda v: (v - a2[0:1, :]) * LOG2E - lz
        th_scr[h] = to_log2(theta)
        t2_scr[h] = to_log2(s2)
        r1_scr[h] = (s1 - a1[0:1, :]) * LOG2E


def _peer_kernel(h2t_ref, xn_ref, mod_ref, gpost_ref, wq_ref, keys_ref, u_ref, vt_ref, out_ref,
                 th_scr, t2_scr, r1_scr, top_scr, acc_scr, a_scr, w_scr, thc_scr, r1c_scr, *, tt, ch):
    b = pl.program_id(0)
    c = pl.program_id(2)
    h2t = h2t_ref[0]

    @pl.when(c == 0)
    def _():
        _peer_score(h2t, wq_ref, keys_ref, th_scr, t2_scr, r1_scr, top_scr, tt)
        acc_scr[...] = jnp.zeros(acc_scr.shape, F32)

    tr, tl = PEER_TILE
    nj = PEER_SUB // PEER_KEYS
    nsub = ch // PEER_SUB
    step_rows = pl.ds(pl.multiple_of(c * (ch // PEER_KEYS), 8), ch // PEER_KEYS)
    thc_scr[...] = th_scr[:, step_rows, :]
    r1c_scr[...] = r1_scr[:, step_rows, :]

    def experts(sc):
        return slice(sc * PEER_SUB, (sc + 1) * PEER_SUB)

    def activations(sc):
        a_scr[sc % 2] = _gelu(jnp.dot(u_ref[experts(sc), :], h2t, preferred_element_type=F32))

    def mix_out(sc):
        acc_scr[...] += jnp.dot(vt_ref[:, experts(sc)], w_scr[sc % 2], preferred_element_type=F32)

    activations(0)
    for sc in range(nsub):
        if sc + 1 < nsub:
            activations(sc + 1)
        if sc > 0:
            mix_out(sc - 1)
        for rt in range(PEER_KEYS // tr):
            for lt in range(tt // tl):
                r = slice(rt * tr, (rt + 1) * tr)
                l = slice(lt * tl, (lt + 1) * tl)
                g = [jnp.zeros(PEER_TILE, F32)] * nj
                for h in range(PEER_HEADS):
                    t2t = t2_scr[h, r, l]
                    for j in range(nj):
                        row = slice(sc * nj + j, sc * nj + j + 1)
                        g[j] = g[j] + jnp.where(t2t >= thc_scr[h, row, l], jnp.exp2(t2t + r1c_scr[h, row, l]), 0.0)
                for j in range(nj):
                    rows = slice(j * PEER_KEYS + rt * tr, j * PEER_KEYS + (rt + 1) * tr)
                    w_scr[sc % 2, rows, l] = (a_scr[sc % 2, rows, l] * g[j]).astype(BF16)
    mix_out(nsub - 1)

    @pl.when(c == pl.num_programs(2) - 1)
    def _():
        y = acc_scr[...].T
        gt2 = mod_ref[pl.ds(b, 1), 5 * D:6 * D]
        out_ref[0] = xn_ref[0] + gt2 * (_rms(y) * gpost_ref[...])


def _peer(h2t, x_new, mod, g_post, wq_t, keys_b, u_b, vt_b, tt, ch):
    B, _, L = h2t.shape
    full = lambda shape: pl.BlockSpec(shape, lambda b, i, c: (0,) * len(shape))
    tile = pl.BlockSpec((1, tt, D), lambda b, i, c: (b, i, 0))
    per_head = pltpu.VMEM((PEER_HEADS, PEER_KEYS, tt), F32)
    nsub = ch // PEER_SUB
    return pl.pallas_call(
        functools.partial(_peer_kernel, tt=tt, ch=ch),
        grid=(B, L // tt, PEER_EXPERTS // ch),
        in_specs=[pl.BlockSpec((1, D, tt), lambda b, i, c: (b, 0, i)), tile, full(mod.shape), full((1, D)),
                  full(wq_t.shape), full(keys_b.shape),
                  pl.BlockSpec((ch, D), lambda b, i, c: (c, 0)),
                  pl.BlockSpec((D, ch), lambda b, i, c: (0, c))],
        out_specs=tile,
        out_shape=jax.ShapeDtypeStruct((B, L, D), F32),
        scratch_shapes=[per_head, per_head, per_head, pltpu.VMEM((2, PEER_TOPK, tt), F32),
                        pltpu.VMEM((D, tt), F32), pltpu.VMEM((2, PEER_SUB, tt), F32),
                        pltpu.VMEM((2, PEER_SUB, tt), BF16),
                        pltpu.VMEM((PEER_HEADS, ch // PEER_KEYS, tt), F32),
                        pltpu.VMEM((PEER_HEADS, ch // PEER_KEYS, tt), F32)],
        compiler_params=pltpu.CompilerParams(dimension_semantics=("arbitrary",) * 3, vmem_limit_bytes=VMEM_LIMIT),
        name="peer",
    )(h2t, x_new, mod, g_post, wq_t, keys_b, u_b, vt_b)


def _rope_tables(L):
    nfreq = DH // 4
    inv = ROPE_BASE ** (-jnp.arange(nfreq, dtype=F32) / nfreq)
    t = jnp.arange(L, dtype=jnp.int32)
    rows = (t // GRID_W).astype(F32)[:, None] * inv[None, :]
    cols = (t % GRID_W).astype(F32)[:, None] * inv[None, :]
    cos64 = jnp.concatenate([jnp.cos(rows), jnp.cos(rows), jnp.cos(cols), jnp.cos(cols)], axis=1)
    sin64 = jnp.concatenate([-jnp.sin(rows), jnp.sin(rows), -jnp.sin(cols), jnp.sin(cols)], axis=1)
    return jnp.tile(cos64, (1, 2)), jnp.tile(sin64, (1, 2))


def kernel(x, c, ctx, c_ctx, w_ada, b_ada, g_pre_mix, g_post_mix, g_pre_ffn, g_post_ffn, w_in, b_gate, lambda_q1, lambda_k1, lambda_q2, lambda_k2, subln_g, sgu_ln_g, sgu_ln_b, sgu_w, sgu_b, w_branch_attn, w_branch_sgu, w_out, peer_w_query, peer_sub_keys, peer_u, peer_v):
    B, L, _ = x.shape
    depth = w_ada.shape[0]
    assert depth == 1 and B <= 7 and L % 512 == 0
    lam_init = 0.8 - 0.6 * math.exp(-0.3 * 0)
    tm = 512 if L % 512 == 0 else 256

    row = lambda a: a[0].reshape(1, -1)
    cc = jnp.concatenate([c, c_ctx[None, :], jnp.zeros((8 - B - 1, D), F32)], axis=0)
    mod, lam = _ada(cc, w_ada[0], row(b_ada), row(lambda_q1), row(lambda_k1), row(lambda_q2), row(lambda_k2), lam_init)

    win_b = w_in[0].astype(BF16)
    cos_t, sin_t = _rope_tables(L)
    sb_b = jnp.broadcast_to(sgu_b[0][:, :, None], (SGU_GROUPS, SGU_CHUNK, SGU_W // SGU_GROUPS))
    q, k, vt, t_sgu, g_attn = _inproj(x, mod, row(g_pre_mix), win_b, row(b_gate), cos_t, sin_t,
                                      row(sgu_ln_g), row(sgu_ln_b), sgu_w[0].astype(BF16), sb_b,
                                      w_branch_sgu[0].astype(BF16), tm)
    kc, vct = _ctxkv(ctx, mod, row(g_pre_mix), win_b, B)
    k_all = jnp.concatenate([kc, k], axis=1)
    vt_all = jnp.concatenate([vct, vt], axis=2)
    tq = 256
    sg_t = jnp.broadcast_to(subln_g[0][:, None], (HEAD_W, tq))
    o = _attn(q, k_all, vt_all, lam, sg_t, lam_init, tq)

    x_new, h2t = _mixout(o, g_attn, t_sgu, x, mod, row(g_post_mix), row(g_pre_ffn),
                         w_branch_attn[0].astype(BF16), w_out[0].astype(BF16), tm)
    return _peer(h2t, x_new, mod, row(g_post_ffn), peer_w_query[0].T.astype(BF16),
                 peer_sub_keys[0].astype(BF16), peer_u[0].astype(BF16), peer_v[0].T.astype(BF16), 512, 2048)
```

```python
import functools
import math

import jax
import jax.numpy as jnp
from jax import lax
from jax.experimental import pallas as pl
from jax.experimental.pallas import tpu as pltpu

F32 = jnp.float32
BF16 = jnp.bfloat16

D = 1024
N_ADA = 6
GRID_W = 64
HEADS = 8
DH = 64
HEAD_W = 2 * DH
ROPE_BASE = 10000.0
SGU_CHUNK = 128
SGU_GROUPS = 8
SGU_W = 1024
Q_W = K_W = V_W = 1024
Z_W = 2 * SGU_W
Z_OFF = Q_W + K_W + V_W
G_OFF = Z_OFF + Z_W
PEER_HEADS = 8
PEER_KEYS = 128
PEER_TOPK = 16
PEER_EXPERTS = PEER_KEYS * PEER_KEYS
RMS_EPS = 1e-6
LN_EPS = 1e-5
LOG2E = 1.4426950408889634
INV_SQRT2 = 0.7071067811865476
NEG_BIG = -1e30
LAM_W = 512

VMEM_LIMIT = 56 * 1024 * 1024


def _gelu(x):
    return 0.5 * x * (1.0 + lax.erf(x * INV_SQRT2))


def _rms(x, eps=RMS_EPS):
    return x * lax.rsqrt(jnp.mean(x * x, axis=-1, keepdims=True) + eps)


def _ada_kernel(cc_ref, w_ref, b_ref, lq1_ref, lk1_ref, lq2_ref, lk2_ref, mod_ref, lam_ref, *, lam_init):
    a = cc_ref[...]
    s = (a * jax.nn.sigmoid(a)).astype(BF16)
    mod_ref[...] = jnp.dot(s, w_ref[...].astype(BF16), preferred_element_type=F32) + b_ref[...]
    d1 = jnp.sum(lq1_ref[...] * lk1_ref[...], axis=-1, keepdims=True)
    d2 = jnp.sum(lq2_ref[...] * lk2_ref[...], axis=-1, keepdims=True)
    lam = jnp.exp(d1) - jnp.exp(d2) + lam_init
    lam_ref[...] = jnp.broadcast_to(lam, lam_ref.shape)


def _ada(cc, w_ada, b_ada, lq1, lk1, lq2, lk2, lam_init):
    tn = 1536
    n = w_ada.shape[1]
    vec = pl.BlockSpec((1, DH), lambda j: (0, 0))
    return pl.pallas_call(
        functools.partial(_ada_kernel, lam_init=lam_init),
        grid=(n // tn,),
        in_specs=[pl.BlockSpec((8, D), lambda j: (0, 0)),
                  pl.BlockSpec((D, tn), lambda j: (0, j)),
                  pl.BlockSpec((1, tn), lambda j: (0, j)),
                  vec, vec, vec, vec],
        out_specs=[pl.BlockSpec((8, tn), lambda j: (0, j)),
                   pl.BlockSpec((8, LAM_W), lambda j: (0, 0))],
        out_shape=[jax.ShapeDtypeStruct((8, n), F32), jax.ShapeDtypeStruct((8, LAM_W), F32)],
        compiler_params=pltpu.CompilerParams(dimension_semantics=("arbitrary",), vmem_limit_bytes=VMEM_LIMIT),
        name="ada",
    )(cc, w_ada, b_ada, lq1, lk1, lq2, lk2)


def _prenorm(x, g_ref, mod_ref, row, col0):
    sh = mod_ref[pl.ds(row, 1), col0:col0 + D]
    sc = mod_ref[pl.ds(row, 1), col0 + D:col0 + 2 * D]
    return _rms(x) * g_ref[...] * (1.0 + sc) + sh


def _inproj_kernel(x_ref, mod_ref, g_ref, win_ref, bg_ref, cos_ref, sin_ref, lng_ref, lnb_ref,
                   sw_ref, sb_ref, wbs_ref, q_ref, k_ref, vt_ref, t_ref, ga_ref, s_scr, *, tm):
    b = pl.program_id(0)
    h = _prenorm(x_ref[0], g_ref, mod_ref, b, 0).astype(BF16)

    lane = lax.broadcasted_iota(jnp.int32, (tm, 128), 1)
    first = (lane & 16) == 0
    cosv = cos_ref[...]
    sinv = sin_ref[...]

    def rope(t):
        swapped = jnp.where(first, pltpu.roll(t, 112, 1), pltpu.roll(t, 16, 1))
        return t * cosv + swapped * sinv

    q = jnp.dot(h, win_ref[:, 0:Q_W], preferred_element_type=F32)
    for j in range(Q_W // 128):
        q_ref[0, :, j * 128:(j + 1) * 128] = (rope(q[:, j * 128:(j + 1) * 128]) * (DH ** -0.5 * LOG2E)).astype(BF16)
    k = jnp.dot(h, win_ref[:, Q_W:Q_W + K_W], preferred_element_type=F32)
    for j in range(K_W // 128):
        k_ref[0, :, j * 128:(j + 1) * 128] = rope(k[:, j * 128:(j + 1) * 128]).astype(BF16)
    vt_ref[0] = jnp.dot(h, win_ref[:, Q_W + K_W:Z_OFF], preferred_element_type=F32).T.astype(BF16)

    z = _gelu(jnp.dot(h, win_ref[:, Z_OFF:G_OFF], preferred_element_type=F32))
    u = z[:, :SGU_W]
    vv = z[:, SGU_W:]
    mu = jnp.mean(vv, axis=-1, keepdims=True)
    xc = vv - mu
    vln = (xc * lax.rsqrt(jnp.mean(xc * xc, axis=-1, keepdims=True) + LN_EPS) * lng_ref[...] + lnb_ref[...]).astype(BF16)
    nc = tm // SGU_CHUNK
    gw = SGU_W // SGU_GROUPS
    for g in range(SGU_GROUPS):
        rhs = jnp.concatenate([vln[n * SGU_CHUNK:(n + 1) * SGU_CHUNK, g * gw:(g + 1) * gw] for n in range(nc)], axis=1)
        mixed = jnp.dot(sw_ref[g], rhs, preferred_element_type=F32)
        for n in range(nc):
            rows = slice(n * SGU_CHUNK, (n + 1) * SGU_CHUNK)
            cols = slice(g * gw, (g + 1) * gw)
            s_scr[rows, cols] = (u[rows, cols] * (mixed[:, n * gw:(n + 1) * gw] + sb_ref[g])).astype(BF16)
    y_sgu = jnp.dot(s_scr[...], wbs_ref[...], preferred_element_type=F32)

    gate = jax.nn.sigmoid(jnp.dot(h, win_ref[:, G_OFF:G_OFF + 2 * D], preferred_element_type=F32) + bg_ref[...])
    ga_ref[0] = gate[:, :D].astype(BF16)
    t_ref[0] = (gate[:, D:] * y_sgu).astype(BF16)


def _inproj(x, mod, g_pre, win_b, b_gate, cos_t, sin_t, ln_g, ln_b, sw_b, sb_b, wbs_b, tm):
    B, L, _ = x.shape
    full = lambda shape: pl.BlockSpec(shape, lambda b, i: (0,) * len(shape))
    tile = pl.BlockSpec((1, tm, D), lambda b, i: (b, i, 0))
    tile_t = pl.BlockSpec((1, D, tm), lambda b, i: (b, 0, i))
    out = jax.ShapeDtypeStruct((B, L, D), BF16)
    out_t = jax.ShapeDtypeStruct((B, D, L), BF16)
    return pl.pallas_call(
        functools.partial(_inproj_kernel, tm=tm),
        grid=(B, L // tm),
        in_specs=[tile, full(mod.shape), full((1, D)), full(win_b.shape), full((1, 2 * D)),
                  pl.BlockSpec((tm, 128), lambda b, i: (i, 0)), pl.BlockSpec((tm, 128), lambda b, i: (i, 0)),
                  full((1, SGU_W)), full((1, SGU_W)), full(sw_b.shape), full(sb_b.shape), full(wbs_b.shape)],
        out_specs=[tile, tile, tile_t, tile, tile],
        out_shape=[out, out, out_t, out, out],
        scratch_shapes=[pltpu.VMEM((tm, SGU_W), BF16)],
        compiler_params=pltpu.CompilerParams(dimension_semantics=("arbitrary", "arbitrary"),
                                             vmem_limit_bytes=VMEM_LIMIT),
        name="inproj",
    )(x, mod, g_pre, win_b, b_gate, cos_t, sin_t, ln_g, ln_b, sw_b, sb_b, wbs_b)


def _ctxkv_kernel(ctx_ref, mod_ref, g_ref, wk_ref, wv_ref, k_ref, vt_ref, *, ctx_row):
    h = _prenorm(ctx_ref[0], g_ref, mod_ref, ctx_row, 0).astype(BF16)
    k_ref[0] = jnp.dot(h, wk_ref[...], preferred_element_type=F32).astype(BF16)
    vt_ref[0] = jnp.dot(h, wv_ref[...], preferred_element_type=F32).T.astype(BF16)


def _ctxkv(ctx, mod, g_pre, win_b, ctx_row):
    B, Lc, _ = ctx.shape
    tile = pl.BlockSpec((1, Lc, D), lambda b: (b, 0, 0))
    return pl.pallas_call(
        functools.partial(_ctxkv_kernel, ctx_row=ctx_row),
        grid=(B,),
        in_specs=[tile, pl.BlockSpec(mod.shape, lambda b: (0, 0)), pl.BlockSpec((1, D), lambda b: (0, 0)),
                  pl.BlockSpec((D, K_W), lambda b: (0, Q_W // K_W)),
                  pl.BlockSpec((D, V_W), lambda b: (0, (Q_W + K_W) // V_W))],
        out_specs=[tile, pl.BlockSpec((1, D, Lc), lambda b: (b, 0, 0))],
        out_shape=[jax.ShapeDtypeStruct((B, Lc, D), BF16), jax.ShapeDtypeStruct((B, D, Lc), BF16)],
        compiler_params=pltpu.CompilerParams(dimension_semantics=("arbitrary",), vmem_limit_bytes=VMEM_LIMIT),
        name="ctxkv",
    )(ctx, mod, g_pre, win_b, win_b)


ONES_ROWS = 16


def _attn_kernel(q_ref, k_ref, vt_ref, lam_ref, sgt_ref, o_ref, s_scr, *, tq, tk, nk, post_scale):
    q = q_ref[0]
    lane = lax.broadcasted_iota(jnp.int32, (tq, HEAD_W), 1)
    zero = jnp.zeros_like(q)
    qs = jnp.concatenate([jnp.where(lane < DH, q, zero), jnp.where(lane >= DH, q, zero)], axis=0)
    ones = jnp.ones((ONES_ROWS, tk), BF16)

    def scores(j):
        kb = k_ref[0, j * tk:(j + 1) * tk, :]
        st = lax.dot_general(kb, qs, (((1,), (1,)), ((), ())), preferred_element_type=F32)
        s_scr[j % 2] = st
        return jnp.max(st, axis=0, keepdims=True)

    m = jnp.full((1, 2 * tq), NEG_BIG, F32)
    acc = jnp.zeros((HEAD_W + ONES_ROWS, 2 * tq), F32)
    blk_max = scores(0)
    for j in range(nk):
        nxt_max = scores(j + 1) if j + 1 < nk else None
        m_new = jnp.maximum(m, blk_max)
        alpha = jnp.exp2(m - m_new)
        pt = jnp.exp2(s_scr[j % 2] - m_new).astype(BF16)
        vte = jnp.concatenate([vt_ref[0, :, j * tk:(j + 1) * tk], ones], axis=0)
        acc = alpha * acc + jnp.dot(vte, pt, preferred_element_type=F32)
        m, blk_max = m_new, nxt_max

    o1 = acc[:HEAD_W, :tq] / acc[HEAD_W:HEAD_W + 1, :tq]
    o2 = acc[:HEAD_W, tq:] / acc[HEAD_W:HEAD_W + 1, tq:]
    ot = o1 - lam_ref[0:1, 0:tq] * o2
    on = ot * lax.rsqrt(jnp.mean(ot * ot, axis=0, keepdims=True) + RMS_EPS) * sgt_ref[...] * post_scale
    o_ref[0] = on.T.astype(BF16)


def _kv_tile(lk):
    best = 128
    for t in range(128, 1537, 128):
        if lk % t == 0:
            best = t
    return best


def _attn(q, k_all, vt_all, lam, sg_t, lam_init, tq):
    B, L, _ = q.shape
    Lk = k_all.shape[1]
    tk = _kv_tile(Lk)
    return pl.pallas_call(
        functools.partial(_attn_kernel, tq=tq, tk=tk, nk=Lk // tk, post_scale=1.0 - lam_init),
        grid=(B, HEADS, L // tq),
        in_specs=[pl.BlockSpec((1, tq, HEAD_W), lambda b, h, i: (b, i, h)),
                  pl.BlockSpec((1, Lk, HEAD_W), lambda b, h, i: (b, 0, h)),
                  pl.BlockSpec((1, HEAD_W, Lk), lambda b, h, i: (b, h, 0)),
                  pl.BlockSpec(lam.shape, lambda b, h, i: (0, 0)),
                  pl.BlockSpec((HEAD_W, tq), lambda b, h, i: (0, 0))],
        out_specs=pl.BlockSpec((1, tq, HEAD_W), lambda b, h, i: (b, i, h)),
        out_shape=jax.ShapeDtypeStruct((B, L, D), BF16),
        scratch_shapes=[pltpu.VMEM((2, tk, 2 * tq), F32)],
        compiler_params=pltpu.CompilerParams(dimension_semantics=("arbitrary",) * 3, vmem_limit_bytes=VMEM_LIMIT),
        name="attn",
    )(q, k_all, vt_all, lam, sg_t)


def _mixout_kernel(o_ref, ga_ref, t_ref, x_ref, mod_ref, gpost_ref, gpre_ref, wba_ref, wout_ref, xn_ref, h2t_ref):
    b = pl.program_id(0)
    y_attn = jnp.dot(o_ref[0], wba_ref[...], preferred_element_type=F32)
    merged = (ga_ref[0].astype(F32) * y_attn + t_ref[0].astype(F32)).astype(BF16)
    y = jnp.dot(merged, wout_ref[...], preferred_element_type=F32)
    gt1 = mod_ref[pl.ds(b, 1), 2 * D:3 * D]
    x_new = x_ref[0] + gt1 * (_rms(y) * gpost_ref[...])
    xn_ref[0] = x_new
    h2 = _prenorm(x_new, gpre_ref, mod_ref, b, 3 * D)
    h2t_ref[0] = h2.T.astype(BF16)


def _mixout(o, ga, t, x, mod, g_post, g_pre_ffn, wba_b, wout_b, tm):
    B, L, _ = x.shape
    full = lambda shape: pl.BlockSpec(shape, lambda b, i: (0,) * len(shape))
    tile = pl.BlockSpec((1, tm, D), lambda b, i: (b, i, 0))
    return pl.pallas_call(
        _mixout_kernel,
        grid=(B, L // tm),
        in_specs=[tile, tile, tile, tile, full(mod.shape), full((1, D)), full((1, D)), full((D, D)), full((D, D))],
        out_specs=[tile, pl.BlockSpec((1, D, tm), lambda b, i: (b, 0, i))],
        out_shape=[jax.ShapeDtypeStruct((B, L, D), F32), jax.ShapeDtypeStruct((B, D, L), BF16)],
        compiler_params=pltpu.CompilerParams(dimension_semantics=("arbitrary", "arbitrary"),
                                             vmem_limit_bytes=VMEM_LIMIT),
        name="mixout",
    )(o, ga, t, x, mod, g_post, g_pre_ffn, wba_b, wout_b)


PEER_SUB = 4 * PEER_KEYS
PEER_TILE = (64, 128)


def _sorting_network(n):
    pairs, p = [], 1
    while p < n:
        k = p
        while k >= 1:
            for j in range(k % p, n - k, 2 * k):
                for i in range(min(k, n - j - k)):
                    if (i + j) // (2 * p) == (i + j + k) // (2 * p):
                        pairs.append((i + j, i + j + k))
            k //= 2
        p *= 2
    return pairs


def _top16_desc(s, top_scr, p):
    n = PEER_KEYS // 8
    lists = [s[8 * k:8 * (k + 1), :] for k in range(n)]
    for i, j in _sorting_network(n):
        lists[i], lists[j] = jnp.maximum(lists[i], lists[j]), jnp.minimum(lists[i], lists[j])
    for r in range(PEER_TOPK):
        m = jnp.max(lists[0], axis=0, keepdims=True)
        top_scr[p, r:r + 1, :] = m
        if r + 1 < PEER_TOPK:
            hit = lists[0] == m
            for i in range(n - 1 - r):
                lists[i] = jnp.where(hit, lists[i + 1], lists[i])


def _peer_score(h2t, wq_ref, keys_ref, th_scr, t2_scr, r1_scr, top_scr, tt):
    qt = jnp.dot(wq_ref[...], h2t, preferred_element_type=F32)
    neg_inf = jnp.float32(-jnp.inf)
    for h in range(PEER_HEADS):
        halves = []
        for p in range(2):
            r0 = (2 * h + p) * PEER_KEYS
            qhp = qt[r0:r0 + PEER_KEYS, :]
            qn = qhp * lax.rsqrt(jnp.mean(qhp * qhp, axis=0, keepdims=True) + RMS_EPS)
            s = jnp.dot(keys_ref[p], qn.astype(BF16), preferred_element_type=F32)
            halves.append(s)
            _top16_desc(s, top_scr, p)
        s1, s2 = halves
        a1 = top_scr[0]
        a2 = top_scr[1]
        cands = [a1[0:1, :] + a2]
        for j in range(1, 8):
            cands.append(a1[j:j + 1, :] + a2[0:8, :])
        cands.append(a1[8:16, :] + a2[0:1, :])
        work = list(cands)
        tau = None
        for r in range(PEER_TOPK):
            m = functools.reduce(jnp.maximum, [jnp.max(c, axis=0, keepdims=True) for c in work])
            if r + 1 < PEER_TOPK:
                work = [jnp.where(c == m, neg_inf, c) for c in work]
            else:
                tau = m
        cmax = a1[0:1, :] + a2[0:1, :]
        zsum = functools.reduce(
            lambda a, c: a + c,
            [jnp.sum(jnp.where(c >= tau, jnp.exp(c - cmax), 0.0), axis=0, keepdims=True) for c in cands])
        theta = jnp.full((PEER_KEYS, tt), jnp.inf, F32)
        for k in range(PEER_TOPK):
            a2k = a2[k:k + 1, :]
            bound = jnp.min(jnp.where(a1 + a2k >= tau, a1, jnp.inf), axis=0, keepdims=True)
            theta = jnp.where(s1 >= bound, a2k, theta)
        lz = jnp.log2(zsum)
        to_log2 = lambda v: (v - a2[0:1, :]) * LOG2E - lz
        th_scr[h] = to_log2(theta)
        t2_scr[h] = to_log2(s2)
        r1_scr[h] = (s1 - a1[0:1, :]) * LOG2E


def _peer_kernel(h2t_ref, xn_ref, mod_ref, gpost_ref, wq_ref, keys_ref, u_ref, vt_ref, out_ref,
                 th_scr, t2_scr, r1_scr, top_scr, acc_scr, a_scr, w_scr, thc_scr, r1c_scr, *, tt, ch):
    b = pl.program_id(0)
    c = pl.program_id(2)
    h2t = h2t_ref[0]

    @pl.when(c == 0)
    def _():
        _peer_score(h2t, wq_ref, keys_ref, th_scr, t2_scr, r1_scr, top_scr, tt)
        acc_scr[...] = jnp.zeros(acc_scr.shape, F32)

    tr, tl = PEER_TILE
    nj = PEER_SUB // PEER_KEYS
    nsub = ch // PEER_SUB
    step_rows = pl.ds(pl.multiple_of(c * (ch // PEER_KEYS), 8), ch // PEER_KEYS)
    thc_scr[...] = th_scr[:, step_rows, :]
    r1c_scr[...] = r1_scr[:, step_rows, :]

    def experts(sc):
        return slice(sc * PEER_SUB, (sc + 1) * PEER_SUB)

    def activations(sc):
        a_scr[sc % 2] = _gelu(jnp.dot(u_ref[experts(sc), :], h2t, preferred_element_type=F32))

    def mix_out(sc):
        acc_scr[...] += jnp.dot(vt_ref[:, experts(sc)], w_scr[sc % 2], preferred_element_type=F32)

    activations(0)
    for sc in range(nsub):
        if sc + 1 < nsub:
            activations(sc + 1)
        if sc > 0:
            mix_out(sc - 1)
        for rt in range(PEER_KEYS // tr):
            for lt in range(tt // tl):
                r = slice(rt * tr, (rt + 1) * tr)
                l = slice(lt * tl, (lt + 1) * tl)
                g = [jnp.zeros(PEER_TILE, F32)] * nj
                for h in range(PEER_HEADS):
                    t2t = t2_scr[h, r, l]
                    for j in range(nj):
                        row = slice(sc * nj + j, sc * nj + j + 1)
                        g[j] = g[j] + jnp.where(t2t >= thc_scr[h, row, l], jnp.exp2(t2t + r1c_scr[h, row, l]), 0.0)
                for j in range(nj):
                    rows = slice(j * PEER_KEYS + rt * tr, j * PEER_KEYS + (rt + 1) * tr)
                    w_scr[sc % 2, rows, l] = (a_scr[sc % 2, rows, l] * g[j]).astype(BF16)
    mix_out(nsub - 1)

    @pl.when(c == pl.num_programs(2) - 1)
    def _():
        y = acc_scr[...].T
        gt2 = mod_ref[pl.ds(b, 1), 5 * D:6 * D]
        out_ref[0] = xn_ref[0] + gt2 * (_rms(y) * gpost_ref[...])


def _peer(h2t, x_new, mod, g_post, wq_t, keys_b, u_b, vt_b, tt, ch):
    B, _, L = h2t.shape
    full = lambda shape: pl.BlockSpec(shape, lambda b, i, c: (0,) * len(shape))
    tile = pl.BlockSpec((1, tt, D), lambda b, i, c: (b, i, 0))
    per_head = pltpu.VMEM((PEER_HEADS, PEER_KEYS, tt), F32)
    nsub = ch // PEER_SUB
    return pl.pallas_call(
        functools.partial(_peer_kernel, tt=tt, ch=ch),
        grid=(B, L // tt, PEER_EXPERTS // ch),
        in_specs=[pl.BlockSpec((1, D, tt), lambda b, i, c: (b, 0, i)), tile, full(mod.shape), full((1, D)),
                  full(wq_t.shape), full(keys_b.shape),
                  pl.BlockSpec((ch, D), lambda b, i, c: (c, 0)),
                  pl.BlockSpec((D, ch), lambda b, i, c: (0, c))],
        out_specs=tile,
        out_shape=jax.ShapeDtypeStruct((B, L, D), F32),
        scratch_shapes=[per_head, per_head, per_head, pltpu.VMEM((2, PEER_TOPK, tt), F32),
                        pltpu.VMEM((D, tt), F32), pltpu.VMEM((2, PEER_SUB, tt), F32),
                        pltpu.VMEM((2, PEER_SUB, tt), BF16),
                        pltpu.VMEM((PEER_HEADS, ch // PEER_KEYS, tt), F32),
                        pltpu.VMEM((PEER_HEADS, ch // PEER_KEYS, tt), F32)],
        compiler_params=pltpu.CompilerParams(dimension_semantics=("arbitrary",) * 3, vmem_limit_bytes=VMEM_LIMIT),
        name="peer",
    )(h2t, x_new, mod, g_post, wq_t, keys_b, u_b, vt_b)


def _rope_tables(L):
    nfreq = DH // 4
    inv = ROPE_BASE ** (-jnp.arange(nfreq, dtype=F32) / nfreq)
    t = jnp.arange(L, dtype=jnp.int32)
    rows = (t // GRID_W).astype(F32)[:, None] * inv[None, :]
    cols = (t % GRID_W).astype(F32)[:, None] * inv[None, :]
    cos64 = jnp.concatenate([jnp.cos(rows), jnp.cos(rows), jnp.cos(cols), jnp.cos(cols)], axis=1)
    sin64 = jnp.concatenate([-jnp.sin(rows), jnp.sin(rows), -jnp.sin(cols), jnp.sin(cols)], axis=1)
    return jnp.tile(cos64, (1, 2)), jnp.tile(sin64, (1, 2))


def kernel(x, c, ctx, c_ctx, w_ada, b_ada, g_pre_mix, g_post_mix, g_pre_ffn, g_post_ffn, w_in, b_gate, lambda_q1, lambda_k1, lambda_q2, lambda_k2, subln_g, sgu_ln_g, sgu_ln_b, sgu_w, sgu_b, w_branch_attn, w_branch_sgu, w_out, peer_w_query, peer_sub_keys, peer_u, peer_v):
    B, L, _ = x.shape
    depth = w_ada.shape[0]
    assert depth == 1 and B <= 7 and L % 512 == 0
    lam_init = 0.8 - 0.6 * math.exp(-0.3 * 0)
    tm = 512 if L % 512 == 0 else 256

    row = lambda a: a[0].reshape(1, -1)
    cc = jnp.concatenate([c, c_ctx[None, :], jnp.zeros((8 - B - 1, D), F32)], axis=0)
    mod, lam = _ada(cc, w_ada[0], row(b_ada), row(lambda_q1), row(lambda_k1), row(lambda_q2), row(lambda_k2), lam_init)

    win_b = w_in[0].astype(BF16)
    cos_t, sin_t = _rope_tables(L)
    sb_b = jnp.broadcast_to(sgu_b[0][:, :, None], (SGU_GROUPS, SGU_CHUNK, SGU_W // SGU_GROUPS))
    q, k, vt, t_sgu, g_attn = _inproj(x, mod, row(g_pre_mix), win_b, row(b_gate), cos_t, sin_t,
                                      row(sgu_ln_g), row(sgu_ln_b), sgu_w[0].astype(BF16), sb_b,
                                      w_branch_sgu[0].astype(BF16), tm)
    kc, vct = _ctxkv(ctx, mod, row(g_pre_mix), win_b, B)
    k_all = jnp.concatenate([kc, k], axis=1)
    vt_all = jnp.concatenate([vct, vt], axis=2)
    tq = 256
    sg_t = jnp.broadcast_to(subln_g[0][:, None], (HEAD_W, tq))
    o = _attn(q, k_all, vt_all, lam, sg_t, lam_init, tq)

    x_new, h2t = _mixout(o, g_attn, t_sgu, x, mod, row(g_post_mix), row(g_pre_ffn),
                         w_branch_attn[0].astype(BF16), w_out[0].astype(BF16), tm)
    return _peer(h2t, x_new, mod, row(g_post_ffn), peer_w_query[0].T.astype(BF16),
                 peer_sub_keys[0].astype(BF16), peer_u[0].astype(BF16), peer_v[0].T.astype(BF16), 512, 2048)
```

```python
import functools
import math

import jax
import jax.numpy as jnp
from jax import lax
from jax.experimental import pallas as pl
from jax.experimental.pallas import tpu as pltpu

F32 = jnp.float32
BF16 = jnp.bfloat16

D = 1024
N_ADA = 6
GRID_W = 64
HEADS = 8
DH = 64
HEAD_W = 2 * DH
ROPE_BASE = 10000.0
SGU_CHUNK = 128
SGU_GROUPS = 8
SGU_W = 1024
Q_W = K_W = V_W = 1024
Z_W = 2 * SGU_W
Z_OFF = Q_W + K_W + V_W
G_OFF = Z_OFF + Z_W
PEER_HEADS = 8
PEER_KEYS = 128
PEER_TOPK = 16
PEER_EXPERTS = PEER_KEYS * PEER_KEYS
RMS_EPS = 1e-6
LN_EPS = 1e-5
LOG2E = 1.4426950408889634
INV_SQRT2 = 0.7071067811865476
NEG_BIG = -1e30
LAM_W = 512

VMEM_LIMIT = 56 * 1024 * 1024


def _gelu(x):
    return 0.5 * x * (1.0 + lax.erf(x * INV_SQRT2))


def _rms(x, eps=RMS_EPS):
    return x * lax.rsqrt(jnp.mean(x * x, axis=-1, keepdims=True) + eps)


def _ada_kernel(cc_ref, w_ref, b_ref, lq1_ref, lk1_ref, lq2_ref, lk2_ref, mod_ref, lam_ref, *, lam_init):
    a = cc_ref[...]
    s = (a * jax.nn.sigmoid(a)).astype(BF16)
    mod_ref[...] = jnp.dot(s, w_ref[...].astype(BF16), preferred_element_type=F32) + b_ref[...]
    d1 = jnp.sum(lq1_ref[...] * lk1_ref[...], axis=-1, keepdims=True)
    d2 = jnp.sum(lq2_ref[...] * lk2_ref[...], axis=-1, keepdims=True)
    lam = jnp.exp(d1) - jnp.exp(d2) + lam_init
    lam_ref[...] = jnp.broadcast_to(lam, lam_ref.shape)


def _ada(cc, w_ada, b_ada, lq1, lk1, lq2, lk2, lam_init):
    tn = 1536
    n = w_ada.shape[1]
    vec = pl.BlockSpec((1, DH), lambda j: (0, 0))
    return pl.pallas_call(
        functools.partial(_ada_kernel, lam_init=lam_init),
        grid=(n // tn,),
        in_specs=[pl.BlockSpec((8, D), lambda j: (0, 0)),
                  pl.BlockSpec((D, tn), lambda j: (0, j)),
                  pl.BlockSpec((1, tn), lambda j: (0, j)),
                  vec, vec, vec, vec],
        out_specs=[pl.BlockSpec((8, tn), lambda j: (0, j)),
                   pl.BlockSpec((8, LAM_W), lambda j: (0, 0))],
        out_shape=[jax.ShapeDtypeStruct((8, n), F32), jax.ShapeDtypeStruct((8, LAM_W), F32)],
        compiler_params=pltpu.CompilerParams(dimension_semantics=("arbitrary",), vmem_limit_bytes=VMEM_LIMIT),
        name="ada",
    )(cc, w_ada, b_ada, lq1, lk1, lq2, lk2)


def _prenorm(x, g_ref, mod_ref, row, col0):
    sh = mod_ref[pl.ds(row, 1), col0:col0 + D]
    sc = mod_ref[pl.ds(row, 1), col0 + D:col0 + 2 * D]
    return _rms(x) * g_ref[...] * (1.0 + sc) + sh


def _inproj_kernel(x_ref, mod_ref, g_ref, win_ref, bg_ref, cos_ref, sin_ref, lng_ref, lnb_ref,
                   sw_ref, sb_ref, wbs_ref, q_ref, k_ref, vt_ref, t_ref, ga_ref, s_scr, *, tm):
    b = pl.program_id(0)
    h = _prenorm(x_ref[0], g_ref, mod_ref, b, 0).astype(BF16)

    lane = lax.broadcasted_iota(jnp.int32, (tm, 128), 1)
    first = (lane & 16) == 0
    cosv = cos_ref[...]
    sinv = sin_ref[...]

    def rope(t):
        swapped = jnp.where(first, pltpu.roll(t, 112, 1), pltpu.roll(t, 16, 1))
        return t * cosv + swapped * sinv

    q = jnp.dot(h, win_ref[:, 0:Q_W], preferred_element_type=F32)
    for j in range(Q_W // 128):
        q_ref[0, :, j * 128:(j + 1) * 128] = (rope(q[:, j * 128:(j + 1) * 128]) * (DH ** -0.5 * LOG2E)).astype(BF16)
    k = jnp.dot(h, win_ref[:, Q_W:Q_W + K_W], preferred_element_type=F32)
    for j in range(K_W // 128):
        k_ref[0, :, j * 128:(j + 1) * 128] = rope(k[:, j * 128:(j + 1) * 128]).astype(BF16)
    vt_ref[0] = jnp.dot(h, win_ref[:, Q_W + K_W:Z_OFF], preferred_element_type=F32).T.astype(BF16)

    z = _gelu(jnp.dot(h, win_ref[:, Z_OFF:G_OFF], preferred_element_type=F32))
    u = z[:, :SGU_W]
    vv = z[:, SGU_W:]
    mu = jnp.mean(vv, axis=-1, keepdims=True)
    xc = vv - mu
    vln = (xc * lax.rsqrt(jnp.mean(xc * xc, axis=-1, keepdims=True) + LN_EPS) * lng_ref[...] + lnb_ref[...]).astype(BF16)
    nc = tm // SGU_CHUNK
    gw = SGU_W // SGU_GROUPS
    for g in range(SGU_GROUPS):
        rhs = jnp.concatenate([vln[n * SGU_CHUNK:(n + 1) * SGU_CHUNK, g * gw:(g + 1) * gw] for n in range(nc)], axis=1)
        mixed = jnp.dot(sw_ref[g], rhs, preferred_element_type=F32)
        for n in range(nc):
            rows = slice(n * SGU_CHUNK, (n + 1) * SGU_CHUNK)
            cols = slice(g * gw, (g + 1) * gw)
            s_scr[rows, cols] = (u[rows, cols] * (mixed[:, n * gw:(n + 1) * gw] + sb_ref[g])).astype(BF16)
    y_sgu = jnp.dot(s_scr[...], wbs_ref[...], preferred_element_type=F32)

    gate = jax.nn.sigmoid(jnp.dot(h, win_ref[:, G_OFF:G_OFF + 2 * D], preferred_element_type=F32) + bg_ref[...])
    ga_ref[0] = gate[:, :D].astype(BF16)
    t_ref[0] = (gate[:, D:] * y_sgu).astype(BF16)


def _inproj(x, mod, g_pre, win_b, b_gate, cos_t, sin_t, ln_g, ln_b, sw_b, sb_b, wbs_b, tm):
    B, L, _ = x.shape
    full = lambda shape: pl.BlockSpec(shape, lambda b, i: (0,) * len(shape))
    tile = pl.BlockSpec((1, tm, D), lambda b, i: (b, i, 0))
    tile_t = pl.BlockSpec((1, D, tm), lambda b, i: (b, 0, i))
    out = jax.ShapeDtypeStruct((B, L, D), BF16)
    out_t = jax.ShapeDtypeStruct((B, D, L), BF16)
    return pl.pallas_call(
        functools.partial(_inproj_kernel, tm=tm),
        grid=(B, L // tm),
        in_specs=[tile, full(mod.shape), full((1, D)), full(win_b.shape), full((1, 2 * D)),
                  pl.BlockSpec((tm, 128), lambda b, i: (i, 0)), pl.BlockSpec((tm, 128), lambda b, i: (i, 0)),
                  full((1, SGU_W)), full((1, SGU_W)), full(sw_b.shape), full(sb_b.shape), full(wbs_b.shape)],
        out_specs=[tile, tile, tile_t, tile, tile],
        out_shape=[out, out, out_t, out, out],
        scratch_shapes=[pltpu.VMEM((tm, SGU_W), BF16)],
        compiler_params=pltpu.CompilerParams(dimension_semantics=("arbitrary", "arbitrary"),
                                             vmem_limit_bytes=VMEM_LIMIT),
        name="inproj",
    )(x, mod, g_pre, win_b, b_gate, cos_t, sin_t, ln_g, ln_b, sw_b, sb_b, wbs_b)


def _ctxkv_kernel(ctx_ref, mod_ref, g_ref, wk_ref, wv_ref, k_ref, vt_ref, *, ctx_row):
    h = _prenorm(ctx_ref[0], g_ref, mod_ref, ctx_row, 0).astype(BF16)
    k_ref[0] = jnp.dot(h, wk_ref[...], preferred_element_type=F32).astype(BF16)
    vt_ref[0] = jnp.dot(h, wv_ref[...], preferred_element_type=F32).T.astype(BF16)


def _ctxkv(ctx, mod, g_pre, win_b, ctx_row):
    B, Lc, _ = ctx.shape
    tile = pl.BlockSpec((1, Lc, D), lambda b: (b, 0, 0))
    return pl.pallas_call(
        functools.partial(_ctxkv_kernel, ctx_row=ctx_row),
        grid=(B,),
        in_specs=[tile, pl.BlockSpec(mod.shape, lambda b: (0, 0)), pl.BlockSpec((1, D), lambda b: (0, 0)),
                  pl.BlockSpec((D, K_W), lambda b: (0, Q_W // K_W)),
                  pl.BlockSpec((D, V_W), lambda b: (0, (Q_W + K_W) // V_W))],
        out_specs=[tile, pl.BlockSpec((1, D, Lc), lambda b: (b, 0, 0))],
        out_shape=[jax.ShapeDtypeStruct((B, Lc, D), BF16), jax.ShapeDtypeStruct((B, D, Lc), BF16)],
        compiler_params=pltpu.CompilerParams(dimension_semantics=("arbitrary",), vmem_limit_bytes=VMEM_LIMIT),
        name="ctxkv",
    )(ctx, mod, g_pre, win_b, win_b)


ONES_ROWS = 16
KV_TILE_MAX = 2816


def _attn_kernel(q_ref, k_ref, vt_ref, lam_ref, sgt_ref, o_ref, s_scr, *, tq, tk, nk, post_scale):
    q = q_ref[0]
    lane = lax.broadcasted_iota(jnp.int32, (tq, HEAD_W), 1)
    zero = jnp.zeros_like(q)
    qs = jnp.concatenate([jnp.where(lane < DH, q, zero), jnp.where(lane >= DH, q, zero)], axis=0)
    ones = jnp.ones((ONES_ROWS, tk), BF16)

    def scores(j):
        kb = k_ref[0, j * tk:(j + 1) * tk, :]
        st = lax.dot_general(kb, qs, (((1,), (1,)), ((), ())), preferred_element_type=F32)
        s_scr[j % 2] = st
        return jnp.max(st, axis=0, keepdims=True)

    m = jnp.full((1, 2 * tq), NEG_BIG, F32)
    acc = jnp.zeros((HEAD_W + ONES_ROWS, 2 * tq), F32)
    blk_max = scores(0)
    for j in range(nk):
        nxt_max = scores(j + 1) if j + 1 < nk else None
        m_new = jnp.maximum(m, blk_max)
        alpha = jnp.exp2(m - m_new)
        pt = jnp.exp2(s_scr[j % 2] - m_new).astype(BF16)
        vte = jnp.concatenate([vt_ref[0, :, j * tk:(j + 1) * tk], ones], axis=0)
        acc = alpha * acc + jnp.dot(vte, pt, preferred_element_type=F32)
        m, blk_max = m_new, nxt_max

    o1 = acc[:HEAD_W, :tq] / acc[HEAD_W:HEAD_W + 1, :tq]
    o2 = acc[:HEAD_W, tq:] / acc[HEAD_W:HEAD_W + 1, tq:]
    ot = o1 - lam_ref[0:1, 0:tq] * o2
    on = ot * lax.rsqrt(jnp.mean(ot * ot, axis=0, keepdims=True) + RMS_EPS) * sgt_ref[...] * post_scale
    o_ref[0] = on.T.astype(BF16)


def _kv_tile(lk):
    best = 128
    for t in range(128, KV_TILE_MAX + 1, 128):
        if lk % t == 0:
            best = t
    return best


def _attn(q, k_all, vt_all, lam, sg_t, lam_init, tq):
    B, L, _ = q.shape
    Lk = k_all.shape[1]
    tk = _kv_tile(Lk)
    return pl.pallas_call(
        functools.partial(_attn_kernel, tq=tq, tk=tk, nk=Lk // tk, post_scale=1.0 - lam_init),
        grid=(B, HEADS, L // tq),
        in_specs=[pl.BlockSpec((1, tq, HEAD_W), lambda b, h, i: (b, i, h)),
                  pl.BlockSpec((1, Lk, HEAD_W), lambda b, h, i: (b, 0, h)),
                  pl.BlockSpec((1, HEAD_W, Lk), lambda b, h, i: (b, h, 0)),
                  pl.BlockSpec(lam.shape, lambda b, h, i: (0, 0)),
                  pl.BlockSpec((HEAD_W, tq), lambda b, h, i: (0, 0))],
        out_specs=pl.BlockSpec((1, tq, HEAD_W), lambda b, h, i: (b, i, h)),
        out_shape=jax.ShapeDtypeStruct((B, L, D), BF16),
        scratch_shapes=[pltpu.VMEM((2, tk, 2 * tq), F32)],
        compiler_params=pltpu.CompilerParams(dimension_semantics=("arbitrary",) * 3, vmem_limit_bytes=VMEM_LIMIT),
        name="attn",
    )(q, k_all, vt_all, lam, sg_t)


def _mixout_kernel(o_ref, ga_ref, t_ref, x_ref, mod_ref, gpost_ref, gpre_ref, wba_ref, wout_ref, xn_ref, h2t_ref):
    b = pl.program_id(0)
    y_attn = jnp.dot(o_ref[0], wba_ref[...], preferred_element_type=F32)
    merged = (ga_ref[0].astype(F32) * y_attn + t_ref[0].astype(F32)).astype(BF16)
    y = jnp.dot(merged, wout_ref[...], preferred_element_type=F32)
    gt1 = mod_ref[pl.ds(b, 1), 2 * D:3 * D]
    x_new = x_ref[0] + gt1 * (_rms(y) * gpost_ref[...])
    xn_ref[0] = x_new
    h2 = _prenorm(x_new, gpre_ref, mod_ref, b, 3 * D)
    h2t_ref[0] = h2.T.astype(BF16)


def _mixout(o, ga, t, x, mod, g_post, g_pre_ffn, wba_b, wout_b, tm):
    B, L, _ = x.shape
    full = lambda shape: pl.BlockSpec(shape, lambda b, i: (0,) * len(shape))
    tile = pl.BlockSpec((1, tm, D), lambda b, i: (b, i, 0))
    return pl.pallas_call(
        _mixout_kernel,
        grid=(B, L // tm),
        in_specs=[tile, tile, tile, tile, full(mod.shape), full((1, D)), full((1, D)), full((D, D)), full((D, D))],
        out_specs=[tile, pl.BlockSpec((1, D, tm), lambda b, i: (b, 0, i))],
        out_shape=[jax.ShapeDtypeStruct((B, L, D), F32), jax.ShapeDtypeStruct((B, D, L), BF16)],
        compiler_params=pltpu.CompilerParams(dimension_semantics=("arbitrary", "arbitrary"),
                                             vmem_limit_bytes=VMEM_LIMIT),
        name="mixout",
    )(o, ga, t, x, mod, g_post, g_pre_ffn, wba_b, wout_b)


PEER_SUB = 4 * PEER_KEYS
PEER_TILE = (64, 128)


def _sorting_network(n):
    pairs, p = [], 1
    while p < n:
        k = p
        while k >= 1:
            for j in range(k % p, n - k, 2 * k):
                for i in range(min(k, n - j - k)):
                    if (i + j) // (2 * p) == (i + j + k) // (2 * p):
                        pairs.append((i + j, i + j + k))
            k //= 2
        p *= 2
    return pairs


def _col_reduce8(x, op):
    for shift in (4, 2, 1):
        x = op(x, pltpu.roll(x, shift, 0))
    return x


def _slabs(x):
    return [x[8 * k:8 * (k + 1), :] for k in range(x.shape[0] // 8)]


def _top16_desc(s, top_scr, p):
    lists = _slabs(s)
    n = len(lists)
    for i, j in _sorting_network(n):
        lists[i], lists[j] = jnp.maximum(lists[i], lists[j]), jnp.minimum(lists[i], lists[j])
    for r in range(PEER_TOPK):
        m = _col_reduce8(lists[0], jnp.maximum)
        top_scr[p, r:r + 1, :] = m[0:1, :]
        if r + 1 < PEER_TOPK:
            hit = lists[0] == m
            for i in range(n - 1 - r):
                lists[i] = jnp.where(hit, lists[i + 1], lists[i])


def _peer_score(h2t, wq_ref, keys_ref, th_scr, t2_scr, r1_scr, top_scr, tt):
    qt = jnp.dot(wq_ref[...], h2t, preferred_element_type=F32)
    neg_inf = jnp.float32(-jnp.inf)
    rep = lambda p, k: jnp.broadcast_to(top_scr[p, k:k + 1, :], (8, tt))
    for h in range(PEER_HEADS):
        halves = []
        for p in range(2):
            r0 = (2 * h + p) * PEER_KEYS
            qhp = qt[r0:r0 + PEER_KEYS, :]
            qn = qhp * lax.rsqrt(jnp.mean(qhp * qhp, axis=0, keepdims=True) + RMS_EPS)
            s = jnp.dot(keys_ref[p], qn.astype(BF16), preferred_element_type=F32)
            halves.append(_slabs(s))
            _top16_desc(s, top_scr, p)
        s1, s2 = halves
        a1_lo, a1_hi = top_scr[0, 0:8, :], top_scr[0, 8:16, :]
        a2_lo, a2_hi = top_scr[1, 0:8, :], top_scr[1, 8:16, :]
        cands = [rep(0, 0) + a2_lo, rep(0, 0) + a2_hi, a1_hi + rep(1, 0)]
        cands += [rep(0, j) + a2_lo for j in range(1, 8)]
        work = list(cands)
        tau = None
        for r in range(PEER_TOPK):
            m = _col_reduce8(functools.reduce(jnp.maximum, work), jnp.maximum)
            if r + 1 < PEER_TOPK:
                work = [jnp.where(c == m, neg_inf, c) for c in work]
            else:
                tau = m
        cmax = rep(0, 0) + rep(1, 0)
        zsum = _col_reduce8(
            functools.reduce(lambda a, c: a + c, [jnp.where(c >= tau, jnp.exp(c - cmax), 0.0) for c in cands]),
            lambda a, c: a + c)
        theta = [jnp.full((8, tt), jnp.inf, F32)] * len(s1)
        for k in range(PEER_TOPK):
            a2k = rep(1, k)
            passing = jnp.minimum(jnp.where(a1_lo + a2k >= tau, a1_lo, jnp.inf),
                                  jnp.where(a1_hi + a2k >= tau, a1_hi, jnp.inf))
            bound = _col_reduce8(passing, jnp.minimum)
            theta = [jnp.where(s >= bound, a2k, t) for s, t in zip(s1, theta)]
        lz = jnp.log2(zsum)
        a1_max, a2_max = rep(0, 0), rep(1, 0)
        for i in range(len(s1)):
            rows = slice(8 * i, 8 * (i + 1))
            th_scr[h, rows, :] = (theta[i] - a2_max) * LOG2E - lz
            t2_scr[h, rows, :] = (s2[i] - a2_max) * LOG2E - lz
            r1_scr[h, rows, :] = (s1[i] - a1_max) * LOG2E


def _peer_kernel(h2t_ref, xn_ref, mod_ref, gpost_ref, wq_ref, keys_ref, u_ref, vt_ref, out_ref,
                 th_scr, t2_scr, r1_scr, top_scr, acc_scr, a_scr, w_scr, thc_scr, r1c_scr, *, tt, ch):
    b = pl.program_id(0)
    c = pl.program_id(2)
    h2t = h2t_ref[0]

    @pl.when(c == 0)
    def _():
        _peer_score(h2t, wq_ref, keys_ref, th_scr, t2_scr, r1_scr, top_scr, tt)
        acc_scr[...] = jnp.zeros(acc_scr.shape, F32)

    tr, tl = PEER_TILE
    nj = PEER_SUB // PEER_KEYS
    nsub = ch // PEER_SUB
    step_rows = pl.ds(pl.multiple_of(c * (ch // PEER_KEYS), 8), ch // PEER_KEYS)
    thc_scr[...] = th_scr[:, step_rows, :]
    r1c_scr[...] = r1_scr[:, step_rows, :]

    def experts(sc):
        return slice(sc * PEER_SUB, (sc + 1) * PEER_SUB)

    def activations(sc):
        a_scr[sc % 2] = _gelu(jnp.dot(u_ref[experts(sc), :], h2t, preferred_element_type=F32))

    def mix_out(sc):
        acc_scr[...] += jnp.dot(vt_ref[:, experts(sc)], w_scr[sc % 2], preferred_element_type=F32)

    activations(0)
    for sc in range(nsub):
        if sc + 1 < nsub:
            activations(sc + 1)
        if sc > 0:
            mix_out(sc - 1)
        for rt in range(PEER_KEYS // tr):
            for lt in range(tt // tl):
                r = slice(rt * tr, (rt + 1) * tr)
                l = slice(lt * tl, (lt + 1) * tl)
                g = [None] * nj
                for h in range(PEER_HEADS):
                    t2t = t2_scr[h, r, l]
                    for j in range(nj):
                        row = slice(sc * nj + j, sc * nj + j + 1)
                        gate = jnp.where(t2t >= thc_scr[h, row, l], jnp.exp2(t2t + r1c_scr[h, row, l]), 0.0)
                        g[j] = gate if h == 0 else g[j] + gate
                for j in range(nj):
                    rows = slice(j * PEER_KEYS + rt * tr, j * PEER_KEYS + (rt + 1) * tr)
                    w_scr[sc % 2, rows, l] = (a_scr[sc % 2, rows, l] * g[j]).astype(BF16)
    mix_out(nsub - 1)

    @pl.when(c == pl.num_programs(2) - 1)
    def _():
        y = acc_scr[...].T
        gt2 = mod_ref[pl.ds(b, 1), 5 * D:6 * D]
        out_ref[0] = xn_ref[0] + gt2 * (_rms(y) * gpost_ref[...])


def _peer(h2t, x_new, mod, g_post, wq_t, keys_b, u_b, vt_b, tt, ch):
    B, _, L = h2t.shape
    full = lambda shape: pl.BlockSpec(shape, lambda b, i, c: (0,) * len(shape))
    tile = pl.BlockSpec((1, tt, D), lambda b, i, c: (b, i, 0))
    per_head = pltpu.VMEM((PEER_HEADS, PEER_KEYS, tt), F32)
    nsub = ch // PEER_SUB
    return pl.pallas_call(
        functools.partial(_peer_kernel, tt=tt, ch=ch),
        grid=(B, L // tt, PEER_EXPERTS // ch),
        in_specs=[pl.BlockSpec((1, D, tt), lambda b, i, c: (b, 0, i)), tile, full(mod.shape), full((1, D)),
                  full(wq_t.shape), full(keys_b.shape),
                  pl.BlockSpec((ch, D), lambda b, i, c: (c, 0)),
                  pl.BlockSpec((D, ch), lambda b, i, c: (0, c))],
        out_specs=tile,
        out_shape=jax.ShapeDtypeStruct((B, L, D), F32),
        scratch_shapes=[per_head, per_head, per_head, pltpu.VMEM((2, PEER_TOPK, tt), F32),
                        pltpu.VMEM((D, tt), F32), pltpu.VMEM((2, PEER_SUB, tt), F32),
                        pltpu.VMEM((2, PEER_SUB, tt), BF16),
                        pltpu.VMEM((PEER_HEADS, ch // PEER_KEYS, tt), F32),
                        pltpu.VMEM((PEER_HEADS, ch // PEER_KEYS, tt), F32)],
        compiler_params=pltpu.CompilerParams(dimension_semantics=("arbitrary",) * 3, vmem_limit_bytes=VMEM_LIMIT),
        name="peer",
    )(h2t, x_new, mod, g_post, wq_t, keys_b, u_b, vt_b)


def _rope_tables(L):
    nfreq = DH // 4
    inv = ROPE_BASE ** (-jnp.arange(nfreq, dtype=F32) / nfreq)
    t = jnp.arange(L, dtype=jnp.int32)
    rows = (t // GRID_W).astype(F32)[:, None] * inv[None, :]
    cols = (t % GRID_W).astype(F32)[:, None] * inv[None, :]
    cos64 = jnp.concatenate([jnp.cos(rows), jnp.cos(rows), jnp.cos(cols), jnp.cos(cols)], axis=1)
    sin64 = jnp.concatenate([-jnp.sin(rows), jnp.sin(rows), -jnp.sin(cols), jnp.sin(cols)], axis=1)
    return jnp.tile(cos64, (1, 2)), jnp.tile(sin64, (1, 2))


def kernel(x, c, ctx, c_ctx, w_ada, b_ada, g_pre_mix, g_post_mix, g_pre_ffn, g_post_ffn, w_in, b_gate, lambda_q1, lambda_k1, lambda_q2, lambda_k2, subln_g, sgu_ln_g, sgu_ln_b, sgu_w, sgu_b, w_branch_attn, w_branch_sgu, w_out, peer_w_query, peer_sub_keys, peer_u, peer_v):
    B, L, _ = x.shape
    depth = w_ada.shape[0]
    assert depth == 1 and B <= 7 and L % 512 == 0
    lam_init = 0.8 - 0.6 * math.exp(-0.3 * 0)
    tm = 512 if L % 512 == 0 else 256

    row = lambda a: a[0].reshape(1, -1)
    cc = jnp.concatenate([c, c_ctx[None, :], jnp.zeros((8 - B - 1, D), F32)], axis=0)
    mod, lam = _ada(cc, w_ada[0], row(b_ada), row(lambda_q1), row(lambda_k1), row(lambda_q2), row(lambda_k2), lam_init)

    win_b = w_in[0].astype(BF16)
    cos_t, sin_t = _rope_tables(L)
    sb_b = jnp.broadcast_to(sgu_b[0][:, :, None], (SGU_GROUPS, SGU_CHUNK, SGU_W // SGU_GROUPS))
    q, k, vt, t_sgu, g_attn = _inproj(x, mod, row(g_pre_mix), win_b, row(b_gate), cos_t, sin_t,
                                      row(sgu_ln_g), row(sgu_ln_b), sgu_w[0].astype(BF16), sb_b,
                                      w_branch_sgu[0].astype(BF16), tm)
    kc, vct = _ctxkv(ctx, mod, row(g_pre_mix), win_b, B)
    k_all = jnp.concatenate([kc, k], axis=1)
    vt_all = jnp.concatenate([vct, vt], axis=2)
    tq = 256
    sg_t = jnp.broadcast_to(subln_g[0][:, None], (HEAD_W, tq))
    o = _attn(q, k_all, vt_all, lam, sg_t, lam_init, tq)

    x_new, h2t = _mixout(o, g_attn, t_sgu, x, mod, row(g_post_mix), row(g_pre_ffn),
                         w_branch_attn[0].astype(BF16), w_out[0].astype(BF16), tm)
    return _peer(h2t, x_new, mod, row(g_post_ffn), peer_w_query[0].T.astype(BF16),
                 peer_sub_keys[0].astype(BF16), peer_u[0].astype(BF16), peer_v[0].T.astype(BF16), 512, 2048)
```

```python
import functools
import math

import jax
import jax.numpy as jnp
from jax import lax
from jax.experimental import pallas as pl
from jax.experimental.pallas import tpu as pltpu

F32 = jnp.float32
BF16 = jnp.bfloat16

D = 1024
N_ADA = 6
GRID_W = 64
HEADS = 8
DH = 64
HEAD_W = 2 * DH
ROPE_BASE = 10000.0
SGU_CHUNK = 128
SGU_GROUPS = 8
SGU_W = 1024
Q_W = K_W = V_W = 1024
Z_W = 2 * SGU_W
Z_OFF = Q_W + K_W + V_W
G_OFF = Z_OFF + Z_W
PEER_HEADS = 8
PEER_KEYS = 128
PEER_TOPK = 16
PEER_EXPERTS = PEER_KEYS * PEER_KEYS
RMS_EPS = 1e-6
LN_EPS = 1e-5
LOG2E = 1.4426950408889634
INV_SQRT2 = 0.7071067811865476
NEG_BIG = -1e30
LAM_W = 512

VMEM_LIMIT = 56 * 1024 * 1024


def _gelu(x):
    return 0.5 * x * (1.0 + lax.erf(x * INV_SQRT2))


def _rms(x, eps=RMS_EPS):
    return x * lax.rsqrt(jnp.mean(x * x, axis=-1, keepdims=True) + eps)


def _ada_kernel(cc_ref, w_ref, b_ref, lq1_ref, lk1_ref, lq2_ref, lk2_ref, mod_ref, lam_ref, *, lam_init):
    a = cc_ref[...]
    s = (a * jax.nn.sigmoid(a)).astype(BF16)
    mod_ref[...] = jnp.dot(s, w_ref[...].astype(BF16), preferred_element_type=F32) + b_ref[...]
    d1 = jnp.sum(lq1_ref[...] * lk1_ref[...], axis=-1, keepdims=True)
    d2 = jnp.sum(lq2_ref[...] * lk2_ref[...], axis=-1, keepdims=True)
    lam = jnp.exp(d1) - jnp.exp(d2) + lam_init
    lam_ref[...] = jnp.broadcast_to(lam, lam_ref.shape)


def _ada(cc, w_ada, b_ada, lq1, lk1, lq2, lk2, lam_init):
    tn = 1536
    n = w_ada.shape[1]
    vec = pl.BlockSpec((1, DH), lambda j: (0, 0))
    return pl.pallas_call(
        functools.partial(_ada_kernel, lam_init=lam_init),
        grid=(n // tn,),
        in_specs=[pl.BlockSpec((8, D), lambda j: (0, 0)),
                  pl.BlockSpec((D, tn), lambda j: (0, j)),
                  pl.BlockSpec((1, tn), lambda j: (0, j)),
                  vec, vec, vec, vec],
        out_specs=[pl.BlockSpec((8, tn), lambda j: (0, j)),
                   pl.BlockSpec((8, LAM_W), lambda j: (0, 0))],
        out_shape=[jax.ShapeDtypeStruct((8, n), F32), jax.ShapeDtypeStruct((8, LAM_W), F32)],
        compiler_params=pltpu.CompilerParams(dimension_semantics=("arbitrary",), vmem_limit_bytes=VMEM_LIMIT),
        name="ada",
    )(cc, w_ada, b_ada, lq1, lk1, lq2, lk2)


def _prenorm(x, g_ref, mod_ref, row, col0):
    sh = mod_ref[pl.ds(row, 1), col0:col0 + D]
    sc = mod_ref[pl.ds(row, 1), col0 + D:col0 + 2 * D]
    return _rms(x) * g_ref[...] * (1.0 + sc) + sh


def _inproj_kernel(x_ref, mod_ref, g_ref, win_ref, bg_ref, cos_ref, sin_ref, lng_ref, lnb_ref,
                   sw_ref, sb_ref, wbs_ref, q_ref, k_ref, vt_ref, t_ref, ga_ref, s_scr, *, tm):
    b = pl.program_id(0)
    h = _prenorm(x_ref[0], g_ref, mod_ref, b, 0).astype(BF16)

    lane = lax.broadcasted_iota(jnp.int32, (tm, 128), 1)
    first = (lane & 16) == 0
    cosv = cos_ref[...]
    sinv = sin_ref[...]

    def rope(t):
        swapped = jnp.where(first, pltpu.roll(t, 112, 1), pltpu.roll(t, 16, 1))
        return t * cosv + swapped * sinv

    q = jnp.dot(h, win_ref[:, 0:Q_W], preferred_element_type=F32)
    for j in range(Q_W // 128):
        q_ref[0, :, j * 128:(j + 1) * 128] = (rope(q[:, j * 128:(j + 1) * 128]) * (DH ** -0.5 * LOG2E)).astype(BF16)
    k = jnp.dot(h, win_ref[:, Q_W:Q_W + K_W], preferred_element_type=F32)
    for j in range(K_W // 128):
        k_ref[0, :, j * 128:(j + 1) * 128] = rope(k[:, j * 128:(j + 1) * 128]).astype(BF16)
    vt_ref[0] = jnp.dot(h, win_ref[:, Q_W + K_W:Z_OFF], preferred_element_type=F32).T.astype(BF16)

    z = _gelu(jnp.dot(h, win_ref[:, Z_OFF:G_OFF], preferred_element_type=F32))
    u = z[:, :SGU_W]
    vv = z[:, SGU_W:]
    mu = jnp.mean(vv, axis=-1, keepdims=True)
    xc = vv - mu
    vln = (xc * lax.rsqrt(jnp.mean(xc * xc, axis=-1, keepdims=True) + LN_EPS) * lng_ref[...] + lnb_ref[...]).astype(BF16)
    nc = tm // SGU_CHUNK
    gw = SGU_W // SGU_GROUPS
    for g in range(SGU_GROUPS):
        rhs = jnp.concatenate([vln[n * SGU_CHUNK:(n + 1) * SGU_CHUNK, g * gw:(g + 1) * gw] for n in range(nc)], axis=1)
        mixed = jnp.dot(sw_ref[g], rhs, preferred_element_type=F32)
        for n in range(nc):
            rows = slice(n * SGU_CHUNK, (n + 1) * SGU_CHUNK)
            cols = slice(g * gw, (g + 1) * gw)
            s_scr[rows, cols] = (u[rows, cols] * (mixed[:, n * gw:(n + 1) * gw] + sb_ref[g])).astype(BF16)
    y_sgu = jnp.dot(s_scr[...], wbs_ref[...], preferred_element_type=F32)

    gate = jax.nn.sigmoid(jnp.dot(h, win_ref[:, G_OFF:G_OFF + 2 * D], preferred_element_type=F32) + bg_ref[...])
    ga_ref[0] = gate[:, :D].astype(BF16)
    t_ref[0] = (gate[:, D:] * y_sgu).astype(BF16)


def _inproj(x, mod, g_pre, win_b, b_gate, cos_t, sin_t, ln_g, ln_b, sw_b, sb_b, wbs_b, tm):
    B, L, _ = x.shape
    full = lambda shape: pl.BlockSpec(shape, lambda b, i: (0,) * len(shape))
    tile = pl.BlockSpec((1, tm, D), lambda b, i: (b, i, 0))
    tile_t = pl.BlockSpec((1, D, tm), lambda b, i: (b, 0, i))
    out = jax.ShapeDtypeStruct((B, L, D), BF16)
    out_t = jax.ShapeDtypeStruct((B, D, L), BF16)
    return pl.pallas_call(
        functools.partial(_inproj_kernel, tm=tm),
        grid=(B, L // tm),
        in_specs=[tile, full(mod.shape), full((1, D)), full(win_b.shape), full((1, 2 * D)),
                  pl.BlockSpec((tm, 128), lambda b, i: (i, 0)), pl.BlockSpec((tm, 128), lambda b, i: (i, 0)),
                  full((1, SGU_W)), full((1, SGU_W)), full(sw_b.shape), full(sb_b.shape), full(wbs_b.shape)],
        out_specs=[tile, tile, tile_t, tile, tile],
        out_shape=[out, out, out_t, out, out],
        scratch_shapes=[pltpu.VMEM((tm, SGU_W), BF16)],
        compiler_params=pltpu.CompilerParams(dimension_semantics=("arbitrary", "arbitrary"),
                                             vmem_limit_bytes=VMEM_LIMIT),
        name="inproj",
    )(x, mod, g_pre, win_b, b_gate, cos_t, sin_t, ln_g, ln_b, sw_b, sb_b, wbs_b)


def _ctxkv_kernel(ctx_ref, mod_ref, g_ref, wk_ref, wv_ref, k_ref, vt_ref, *, ctx_row):
    h = _prenorm(ctx_ref[0], g_ref, mod_ref, ctx_row, 0).astype(BF16)
    k_ref[0] = jnp.dot(h, wk_ref[...], preferred_element_type=F32).astype(BF16)
    vt_ref[0] = jnp.dot(h, wv_ref[...], preferred_element_type=F32).T.astype(BF16)


def _ctxkv(ctx, mod, g_pre, win_b, ctx_row):
    B, Lc, _ = ctx.shape
    tile = pl.BlockSpec((1, Lc, D), lambda b: (b, 0, 0))
    return pl.pallas_call(
        functools.partial(_ctxkv_kernel, ctx_row=ctx_row),
        grid=(B,),
        in_specs=[tile, pl.BlockSpec(mod.shape, lambda b: (0, 0)), pl.BlockSpec((1, D), lambda b: (0, 0)),
                  pl.BlockSpec((D, K_W), lambda b: (0, Q_W // K_W)),
                  pl.BlockSpec((D, V_W), lambda b: (0, (Q_W + K_W) // V_W))],
        out_specs=[tile, pl.BlockSpec((1, D, Lc), lambda b: (b, 0, 0))],
        out_shape=[jax.ShapeDtypeStruct((B, Lc, D), BF16), jax.ShapeDtypeStruct((B, D, Lc), BF16)],
        compiler_params=pltpu.CompilerParams(dimension_semantics=("arbitrary",), vmem_limit_bytes=VMEM_LIMIT),
        name="ctxkv",
    )(ctx, mod, g_pre, win_b, win_b)


ONES_ROWS = 16
KV_TILE_MAX = 2816


def _attn_kernel(q_ref, k_ref, vt_ref, lam_ref, sgt_ref, o_ref, s_scr, acc_scr, *, tq, nt, tk, nk, post_scale):
    lane = lax.broadcasted_iota(jnp.int32, (tq, HEAD_W), 1)
    ones = jnp.ones((ONES_ROWS, tk), BF16)

    def stacked_q(t):
        q = q_ref[0, t * tq:(t + 1) * tq, :]
        zero = jnp.zeros_like(q)
        return jnp.concatenate([jnp.where(lane < DH, q, zero), jnp.where(lane >= DH, q, zero)], axis=0)

    def scores(seq):
        t, j = divmod(seq, nk)
        kb = k_ref[0, j * tk:(j + 1) * tk, :]
        st = lax.dot_general(kb, stacked_q(t), (((1,), (1,)), ((), ())), preferred_element_type=F32)
        s_scr[seq % 2] = st
        return jnp.max(st, axis=0, keepdims=True)

    blk_max = scores(0)
    for seq in range(nt * nk):
        t, j = divmod(seq, nk)
        nxt_max = scores(seq + 1) if seq + 1 < nt * nk else None
        if j == 0:
            m = jnp.full((1, 2 * tq), NEG_BIG, F32)
        m_new = jnp.maximum(m, blk_max)
        pt = jnp.exp2(s_scr[seq % 2] - m_new).astype(BF16)
        vte = jnp.concatenate([vt_ref[0, :, j * tk:(j + 1) * tk], ones], axis=0)
        pv = jnp.dot(vte, pt, preferred_element_type=F32)
        acc_scr[...] = pv if j == 0 else jnp.exp2(m - m_new) * acc_scr[...] + pv
        m, blk_max = m_new, nxt_max
        if j == nk - 1:
            acc = acc_scr[...]
            o1 = acc[:HEAD_W, :tq] / acc[HEAD_W:HEAD_W + 1, :tq]
            o2 = acc[:HEAD_W, tq:] / acc[HEAD_W:HEAD_W + 1, tq:]
            ot = o1 - lam_ref[0:1, 0:tq] * o2
            on = ot * lax.rsqrt(jnp.mean(ot * ot, axis=0, keepdims=True) + RMS_EPS) * sgt_ref[...] * post_scale
            o_ref[0, t * tq:(t + 1) * tq, :] = on.T.astype(BF16)


def _kv_tile(lk):
    best = 128
    for t in range(128, KV_TILE_MAX + 1, 128):
        if lk % t == 0:
            best = t
    return best


def _attn(q, k_all, vt_all, lam, sg_t, lam_init, tq, nt):
    B, L, _ = q.shape
    Lk = k_all.shape[1]
    tk = _kv_tile(Lk)
    return pl.pallas_call(
        functools.partial(_attn_kernel, tq=tq, nt=nt, tk=tk, nk=Lk // tk, post_scale=1.0 - lam_init),
        grid=(B, HEADS, L // (nt * tq)),
        in_specs=[pl.BlockSpec((1, nt * tq, HEAD_W), lambda b, h, i: (b, i, h)),
                  pl.BlockSpec((1, Lk, HEAD_W), lambda b, h, i: (b, 0, h)),
                  pl.BlockSpec((1, HEAD_W, Lk), lambda b, h, i: (b, h, 0)),
                  pl.BlockSpec(lam.shape, lambda b, h, i: (0, 0)),
                  pl.BlockSpec((HEAD_W, tq), lambda b, h, i: (0, 0))],
        out_specs=pl.BlockSpec((1, nt * tq, HEAD_W), lambda b, h, i: (b, i, h)),
        out_shape=jax.ShapeDtypeStruct((B, L, D), BF16),
        scratch_shapes=[pltpu.VMEM((2, tk, 2 * tq), F32), pltpu.VMEM((HEAD_W + ONES_ROWS, 2 * tq), F32)],
        compiler_params=pltpu.CompilerParams(dimension_semantics=("arbitrary",) * 3, vmem_limit_bytes=VMEM_LIMIT),
        name="attn",
    )(q, k_all, vt_all, lam, sg_t)


def _mixout_kernel(o_ref, ga_ref, t_ref, x_ref, mod_ref, gpost_ref, gpre_ref, wba_ref, wout_ref, xn_ref, h2t_ref):
    b = pl.program_id(0)
    y_attn = jnp.dot(o_ref[0], wba_ref[...], preferred_element_type=F32)
    merged = (ga_ref[0].astype(F32) * y_attn + t_ref[0].astype(F32)).astype(BF16)
    y = jnp.dot(merged, wout_ref[...], preferred_element_type=F32)
    gt1 = mod_ref[pl.ds(b, 1), 2 * D:3 * D]
    x_new = x_ref[0] + gt1 * (_rms(y) * gpost_ref[...])
    xn_ref[0] = x_new
    h2 = _prenorm(x_new, gpre_ref, mod_ref, b, 3 * D)
    h2t_ref[0] = h2.T.astype(BF16)


def _mixout(o, ga, t, x, mod, g_post, g_pre_ffn, wba_b, wout_b, tm):
    B, L, _ = x.shape
    full = lambda shape: pl.BlockSpec(shape, lambda b, i: (0,) * len(shape))
    tile = pl.BlockSpec((1, tm, D), lambda b, i: (b, i, 0))
    return pl.pallas_call(
        _mixout_kernel,
        grid=(B, L // tm),
        in_specs=[tile, tile, tile, tile, full(mod.shape), full((1, D)), full((1, D)), full((D, D)), full((D, D))],
        out_specs=[tile, pl.BlockSpec((1, D, tm), lambda b, i: (b, 0, i))],
        out_shape=[jax.ShapeDtypeStruct((B, L, D), F32), jax.ShapeDtypeStruct((B, D, L), BF16)],
        compiler_params=pltpu.CompilerParams(dimension_semantics=("arbitrary", "arbitrary"),
                                             vmem_limit_bytes=VMEM_LIMIT),
        name="mixout",
    )(o, ga, t, x, mod, g_post, g_pre_ffn, wba_b, wout_b)


PEER_SUB = 4 * PEER_KEYS
PEER_TILE = (64, 128)


def _sorting_network(n):
    pairs, p = [], 1
    while p < n:
        k = p
        while k >= 1:
            for j in range(k % p, n - k, 2 * k):
                for i in range(min(k, n - j - k)):
                    if (i + j) // (2 * p) == (i + j + k) // (2 * p):
                        pairs.append((i + j, i + j + k))
            k //= 2
        p *= 2
    return pairs


def _col_reduce8(x, op):
    for shift in (4, 2, 1):
        x = op(x, pltpu.roll(x, shift, 0))
    return x


def _slabs(x):
    return [x[8 * k:8 * (k + 1), :] for k in range(x.shape[0] // 8)]


def _top16_desc(s, top_scr, p):
    lists = _slabs(s)
    n = len(lists)
    for i, j in _sorting_network(n):
        lists[i], lists[j] = jnp.maximum(lists[i], lists[j]), jnp.minimum(lists[i], lists[j])
    for r in range(PEER_TOPK):
        m = _col_reduce8(lists[0], jnp.maximum)
        top_scr[p, r:r + 1, :] = m[0:1, :]
        if r + 1 < PEER_TOPK:
            hit = lists[0] == m
            for i in range(n - 1 - r):
                lists[i] = jnp.where(hit, lists[i + 1], lists[i])


def _peer_score(h2t, wq_ref, keys_ref, th_scr, t2_scr, r1_scr, top_scr, tt):
    qt = jnp.dot(wq_ref[...], h2t, preferred_element_type=F32)
    neg_inf = jnp.float32(-jnp.inf)
    rep = lambda p, k: jnp.broadcast_to(top_scr[p, k:k + 1, :], (8, tt))
    for h in range(PEER_HEADS):
        halves = []
        for p in range(2):
            r0 = (2 * h + p) * PEER_KEYS
            qhp = qt[r0:r0 + PEER_KEYS, :]
            qn = qhp * lax.rsqrt(jnp.mean(qhp * qhp, axis=0, keepdims=True) + RMS_EPS)
            s = jnp.dot(keys_ref[p], qn.astype(BF16), preferred_element_type=F32)
            halves.append(_slabs(s))
            _top16_desc(s, top_scr, p)
        s1, s2 = halves
        a1_lo, a1_hi = top_scr[0, 0:8, :], top_scr[0, 8:16, :]
        a2_lo, a2_hi = top_scr[1, 0:8, :], top_scr[1, 8:16, :]
        cands = [rep(0, 0) + a2_lo, rep(0, 0) + a2_hi, a1_hi + rep(1, 0)]
        cands += [rep(0, j) + a2_lo for j in range(1, 8)]
        work = list(cands)
        tau = None
        for r in range(PEER_TOPK):
            m = _col_reduce8(functools.reduce(jnp.maximum, work), jnp.maximum)
            if r + 1 < PEER_TOPK:
                work = [jnp.where(c == m, neg_inf, c) for c in work]
            else:
                tau = m
        cmax = rep(0, 0) + rep(1, 0)
        zsum = _col_reduce8(
            functools.reduce(lambda a, c: a + c, [jnp.where(c >= tau, jnp.exp(c - cmax), 0.0) for c in cands]),
            lambda a, c: a + c)
        theta = [jnp.full((8, tt), jnp.inf, F32)] * len(s1)
        for k in range(PEER_TOPK):
            a2k = rep(1, k)
            passing = jnp.minimum(jnp.where(a1_lo + a2k >= tau, a1_lo, jnp.inf),
                                  jnp.where(a1_hi + a2k >= tau, a1_hi, jnp.inf))
            bound = _col_reduce8(passing, jnp.minimum)
            theta = [jnp.where(s >= bound, a2k, t) for s, t in zip(s1, theta)]
        lz = jnp.log2(zsum)
        a1_max, a2_max = rep(0, 0), rep(1, 0)
        for i in range(len(s1)):
            rows = slice(8 * i, 8 * (i + 1))
            th_scr[h, rows, :] = (theta[i] - a2_max) * LOG2E - lz
            t2_scr[h, rows, :] = (s2[i] - a2_max) * LOG2E - lz
            r1_scr[h, rows, :] = (s1[i] - a1_max) * LOG2E


def _peer_kernel(h2t_ref, xn_ref, mod_ref, gpost_ref, wq_ref, keys_ref, u_ref, vt_ref, out_ref,
                 th_scr, t2_scr, r1_scr, top_scr, acc_scr, a_scr, w_scr, thc_scr, r1c_scr, *, tt, ch):
    b = pl.program_id(0)
    c = pl.program_id(2)
    h2t = h2t_ref[0]

    @pl.when(c == 0)
    def _():
        _peer_score(h2t, wq_ref, keys_ref, th_scr, t2_scr, r1_scr, top_scr, tt)
        acc_scr[...] = jnp.zeros(acc_scr.shape, F32)

    tr, tl = PEER_TILE
    nj = PEER_SUB // PEER_KEYS
    nsub = ch // PEER_SUB
    step_rows = pl.ds(pl.multiple_of(c * (ch // PEER_KEYS), 8), ch // PEER_KEYS)
    thc_scr[...] = th_scr[:, step_rows, :]
    r1c_scr[...] = r1_scr[:, step_rows, :]

    def experts(sc):
        return slice(sc * PEER_SUB, (sc + 1) * PEER_SUB)

    def activations(sc):
        a_scr[sc % 2] = _gelu(jnp.dot(u_ref[experts(sc), :], h2t, preferred_element_type=F32))

    def mix_out(sc):
        acc_scr[...] += jnp.dot(vt_ref[:, experts(sc)], w_scr[sc % 2], preferred_element_type=F32)

    activations(0)
    for sc in range(nsub):
        if sc + 1 < nsub:
            activations(sc + 1)
        if sc > 0:
            mix_out(sc - 1)
        for rt in range(PEER_KEYS // tr):
            for lt in range(tt // tl):
                r = slice(rt * tr, (rt + 1) * tr)
                l = slice(lt * tl, (lt + 1) * tl)
                g = [None] * nj
                for h in range(PEER_HEADS):
                    t2t = t2_scr[h, r, l]
                    for j in range(nj):
                        row = slice(sc * nj + j, sc * nj + j + 1)
                        gate = jnp.where(t2t >= thc_scr[h, row, l], jnp.exp2(t2t + r1c_scr[h, row, l]), 0.0)
                        g[j] = gate if h == 0 else g[j] + gate
                for j in range(nj):
                    rows = slice(j * PEER_KEYS + rt * tr, j * PEER_KEYS + (rt + 1) * tr)
                    w_scr[sc % 2, rows, l] = (a_scr[sc % 2, rows, l] * g[j]).astype(BF16)
    mix_out(nsub - 1)

    @pl.when(c == pl.num_programs(2) - 1)
    def _():
        y = acc_scr[...].T
        gt2 = mod_ref[pl.ds(b, 1), 5 * D:6 * D]
        out_ref[0] = xn_ref[0] + gt2 * (_rms(y) * gpost_ref[...])


def _peer(h2t, x_new, mod, g_post, wq_t, keys_b, u_b, vt_b, tt, ch):
    B, _, L = h2t.shape
    full = lambda shape: pl.BlockSpec(shape, lambda b, i, c: (0,) * len(shape))
    tile = pl.BlockSpec((1, tt, D), lambda b, i, c: (b, i, 0))
    per_head = pltpu.VMEM((PEER_HEADS, PEER_KEYS, tt), F32)
    nsub = ch // PEER_SUB
    return pl.pallas_call(
        functools.partial(_peer_kernel, tt=tt, ch=ch),
        grid=(B, L // tt, PEER_EXPERTS // ch),
        in_specs=[pl.BlockSpec((1, D, tt), lambda b, i, c: (b, 0, i)), tile, full(mod.shape), full((1, D)),
                  full(wq_t.shape), full(keys_b.shape),
                  pl.BlockSpec((ch, D), lambda b, i, c: (c, 0)),
                  pl.BlockSpec((D, ch), lambda b, i, c: (0, c))],
        out_specs=tile,
        out_shape=jax.ShapeDtypeStruct((B, L, D), F32),
        scratch_shapes=[per_head, per_head, per_head, pltpu.VMEM((2, PEER_TOPK, tt), F32),
                        pltpu.VMEM((D, tt), F32), pltpu.VMEM((2, PEER_SUB, tt), F32),
                        pltpu.VMEM((2, PEER_SUB, tt), BF16),
                        pltpu.VMEM((PEER_HEADS, ch // PEER_KEYS, tt), F32),
                        pltpu.VMEM((PEER_HEADS, ch // PEER_KEYS, tt), F32)],
        compiler_params=pltpu.CompilerParams(dimension_semantics=("arbitrary",) * 3, vmem_limit_bytes=VMEM_LIMIT),
        name="peer",
    )(h2t, x_new, mod, g_post, wq_t, keys_b, u_b, vt_b)


def _rope_tables(L):
    nfreq = DH // 4
    inv = ROPE_BASE ** (-jnp.arange(nfreq, dtype=F32) / nfreq)
    t = jnp.arange(L, dtype=jnp.int32)
    rows = (t // GRID_W).astype(F32)[:, None] * inv[None, :]
    cols = (t % GRID_W).astype(F32)[:, None] * inv[None, :]
    cos64 = jnp.concatenate([jnp.cos(rows), jnp.cos(rows), jnp.cos(cols), jnp.cos(cols)], axis=1)
    sin64 = jnp.concatenate([-jnp.sin(rows), jnp.sin(rows), -jnp.sin(cols), jnp.sin(cols)], axis=1)
    return jnp.tile(cos64, (1, 2)), jnp.tile(sin64, (1, 2))


def kernel(x, c, ctx, c_ctx, w_ada, b_ada, g_pre_mix, g_post_mix, g_pre_ffn, g_post_ffn, w_in, b_gate, lambda_q1, lambda_k1, lambda_q2, lambda_k2, subln_g, sgu_ln_g, sgu_ln_b, sgu_w, sgu_b, w_branch_attn, w_branch_sgu, w_out, peer_w_query, peer_sub_keys, peer_u, peer_v):
    B, L, _ = x.shape
    depth = w_ada.shape[0]
    assert depth == 1 and B <= 7 and L % 512 == 0
    lam_init = 0.8 - 0.6 * math.exp(-0.3 * 0)
    tm = 512 if L % 512 == 0 else 256

    row = lambda a: a[0].reshape(1, -1)
    cc = jnp.concatenate([c, c_ctx[None, :], jnp.zeros((8 - B - 1, D), F32)], axis=0)
    mod, lam = _ada(cc, w_ada[0], row(b_ada), row(lambda_q1), row(lambda_k1), row(lambda_q2), row(lambda_k2), lam_init)

    win_b = w_in[0].astype(BF16)
    cos_t, sin_t = _rope_tables(L)
    sb_b = jnp.broadcast_to(sgu_b[0][:, :, None], (SGU_GROUPS, SGU_CHUNK, SGU_W // SGU_GROUPS))
    q, k, vt, t_sgu, g_attn = _inproj(x, mod, row(g_pre_mix), win_b, row(b_gate), cos_t, sin_t,
                                      row(sgu_ln_g), row(sgu_ln_b), sgu_w[0].astype(BF16), sb_b,
                                      w_branch_sgu[0].astype(BF16), tm)
    kc, vct = _ctxkv(ctx, mod, row(g_pre_mix), win_b, B)
    k_all = jnp.concatenate([kc, k], axis=1)
    vt_all = jnp.concatenate([vct, vt], axis=2)
    tq = 256
    sg_t = jnp.broadcast_to(subln_g[0][:, None], (HEAD_W, tq))
    o = _attn(q, k_all, vt_all, lam, sg_t, lam_init, tq, 4 if L % (4 * tq) == 0 else 2)

    x_new, h2t = _mixout(o, g_attn, t_sgu, x, mod, row(g_post_mix), row(g_pre_ffn),
                         w_branch_attn[0].astype(BF16), w_out[0].astype(BF16), tm)
    return _peer(h2t, x_new, mod, row(g_post_ffn), peer_w_query[0].T.astype(BF16),
                 peer_sub_keys[0].astype(BF16), peer_u[0].astype(BF16), peer_v[0].T.astype(BF16), 512, 2048)
```

```python
import functools
import math

import jax
import jax.numpy as jnp
import numpy as np
from jax import lax
from jax.experimental import pallas as pl
from jax.experimental.pallas import tpu as pltpu

F32 = jnp.float32
BF16 = jnp.bfloat16

D = 1024
N_ADA = 6
GRID_W = 64
HEADS = 8
DH = 64
HEAD_W = 2 * DH
ROPE_BASE = 10000.0
SGU_CHUNK = 128
SGU_GROUPS = 8
SGU_W = 1024
Q_W = K_W = V_W = 1024
Z_W = 2 * SGU_W
Z_OFF = Q_W + K_W + V_W
G_OFF = Z_OFF + Z_W
PEER_HEADS = 8
PEER_KEYS = 128
PEER_TOPK = 16
PEER_EXPERTS = PEER_KEYS * PEER_KEYS
RMS_EPS = 1e-6
LN_EPS = 1e-5
LOG2E = 1.4426950408889634
INV_SQRT2 = 0.7071067811865476
NEG_BIG = -1e30
LAM_W = 512

VMEM_LIMIT = 56 * 1024 * 1024


def _gelu(x):
    return 0.5 * x * (1.0 + lax.erf(x * INV_SQRT2))


def _rms(x, eps=RMS_EPS):
    return x * lax.rsqrt(jnp.mean(x * x, axis=-1, keepdims=True) + eps)


def _ada_kernel(cc_ref, w_ref, b_ref, lq1_ref, lk1_ref, lq2_ref, lk2_ref, mod_ref, lam_ref, *, lam_init):
    a = cc_ref[...]
    s = (a * jax.nn.sigmoid(a)).astype(BF16)
    mod_ref[...] = jnp.dot(s, w_ref[...].astype(BF16), preferred_element_type=F32) + b_ref[...]
    d1 = jnp.sum(lq1_ref[...] * lk1_ref[...], axis=-1, keepdims=True)
    d2 = jnp.sum(lq2_ref[...] * lk2_ref[...], axis=-1, keepdims=True)
    lam = jnp.exp(d1) - jnp.exp(d2) + lam_init
    lam_ref[...] = jnp.broadcast_to(lam, lam_ref.shape)


def _ada(cc, w_ada, b_ada, lq1, lk1, lq2, lk2, lam_init):
    tn = 1536
    n = w_ada.shape[1]
    vec = pl.BlockSpec((1, DH), lambda j: (0, 0))
    return pl.pallas_call(
        functools.partial(_ada_kernel, lam_init=lam_init),
        grid=(n // tn,),
        in_specs=[pl.BlockSpec((8, D), lambda j: (0, 0)),
                  pl.BlockSpec((D, tn), lambda j: (0, j)),
                  pl.BlockSpec((1, tn), lambda j: (0, j)),
                  vec, vec, vec, vec],
        out_specs=[pl.BlockSpec((8, tn), lambda j: (0, j)),
                   pl.BlockSpec((8, LAM_W), lambda j: (0, 0))],
        out_shape=[jax.ShapeDtypeStruct((8, n), F32), jax.ShapeDtypeStruct((8, LAM_W), F32)],
        compiler_params=pltpu.CompilerParams(dimension_semantics=("arbitrary",), vmem_limit_bytes=VMEM_LIMIT),
        name="ada",
    )(cc, w_ada, b_ada, lq1, lk1, lq2, lk2)


def _prenorm(x, g_ref, mod_ref, row, col0):
    sh = mod_ref[pl.ds(row, 1), col0:col0 + D]
    sc = mod_ref[pl.ds(row, 1), col0 + D:col0 + 2 * D]
    return _rms(x) * g_ref[...] * (1.0 + sc) + sh


def _inproj_kernel(x_ref, mod_ref, g_ref, win_ref, bg_ref, cos_ref, sin_ref, lng_ref, lnb_ref,
                   sw_ref, sb_ref, wbs_ref, q_ref, k_ref, vt_ref, t_ref, ga_ref, s_scr, *, tm):
    b = pl.program_id(0)
    h = _prenorm(x_ref[0], g_ref, mod_ref, b, 0).astype(BF16)

    lane = lax.broadcasted_iota(jnp.int32, (tm, 128), 1)
    first = (lane & 16) == 0
    cosv = cos_ref[...]
    sinv = sin_ref[...]

    def rope(t):
        swapped = jnp.where(first, pltpu.roll(t, 112, 1), pltpu.roll(t, 16, 1))
        return t * cosv + swapped * sinv

    q = jnp.dot(h, win_ref[:, 0:Q_W], preferred_element_type=F32)
    for j in range(Q_W // 128):
        q_ref[0, :, j * 128:(j + 1) * 128] = (rope(q[:, j * 128:(j + 1) * 128]) * (DH ** -0.5 * LOG2E)).astype(BF16)
    k = jnp.dot(h, win_ref[:, Q_W:Q_W + K_W], preferred_element_type=F32)
    for j in range(K_W // 128):
        k_ref[0, :, j * 128:(j + 1) * 128] = rope(k[:, j * 128:(j + 1) * 128]).astype(BF16)
    vt_ref[0] = jnp.dot(h, win_ref[:, Q_W + K_W:Z_OFF], preferred_element_type=F32).T.astype(BF16)

    z = _gelu(jnp.dot(h, win_ref[:, Z_OFF:G_OFF], preferred_element_type=F32))
    u = z[:, :SGU_W]
    vv = z[:, SGU_W:]
    mu = jnp.mean(vv, axis=-1, keepdims=True)
    xc = vv - mu
    vln = (xc * lax.rsqrt(jnp.mean(xc * xc, axis=-1, keepdims=True) + LN_EPS) * lng_ref[...] + lnb_ref[...]).astype(BF16)
    nc = tm // SGU_CHUNK
    gw = SGU_W // SGU_GROUPS
    for g in range(SGU_GROUPS):
        rhs = jnp.concatenate([vln[n * SGU_CHUNK:(n + 1) * SGU_CHUNK, g * gw:(g + 1) * gw] for n in range(nc)], axis=1)
        mixed = jnp.dot(sw_ref[g], rhs, preferred_element_type=F32)
        for n in range(nc):
            rows = slice(n * SGU_CHUNK, (n + 1) * SGU_CHUNK)
            cols = slice(g * gw, (g + 1) * gw)
            s_scr[rows, cols] = (u[rows, cols] * (mixed[:, n * gw:(n + 1) * gw] + sb_ref[g])).astype(BF16)
    y_sgu = jnp.dot(s_scr[...], wbs_ref[...], preferred_element_type=F32)

    gate = jax.nn.sigmoid(jnp.dot(h, win_ref[:, G_OFF:G_OFF + 2 * D], preferred_element_type=F32) + bg_ref[...])
    ga_ref[0] = gate[:, :D].astype(BF16)
    t_ref[0] = (gate[:, D:] * y_sgu).astype(BF16)


def _inproj(x, mod, g_pre, win_b, b_gate, cos_t, sin_t, ln_g, ln_b, sw_b, sb_b, wbs_b, tm):
    B, L, _ = x.shape
    full = lambda shape: pl.BlockSpec(shape, lambda b, i: (0,) * len(shape))
    tile = pl.BlockSpec((1, tm, D), lambda b, i: (b, i, 0))
    tile_t = pl.BlockSpec((1, D, tm), lambda b, i: (b, 0, i))
    out = jax.ShapeDtypeStruct((B, L, D), BF16)
    out_t = jax.ShapeDtypeStruct((B, D, L), BF16)
    return pl.pallas_call(
        functools.partial(_inproj_kernel, tm=tm),
        grid=(B, L // tm),
        in_specs=[tile, full(mod.shape), full((1, D)), full(win_b.shape), full((1, 2 * D)),
                  pl.BlockSpec((tm, 128), lambda b, i: (i, 0)), pl.BlockSpec((tm, 128), lambda b, i: (i, 0)),
                  full((1, SGU_W)), full((1, SGU_W)), full(sw_b.shape), full(sb_b.shape), full(wbs_b.shape)],
        out_specs=[tile, tile, tile_t, tile, tile],
        out_shape=[out, out, out_t, out, out],
        scratch_shapes=[pltpu.VMEM((tm, SGU_W), BF16)],
        compiler_params=pltpu.CompilerParams(dimension_semantics=("arbitrary", "arbitrary"),
                                             vmem_limit_bytes=VMEM_LIMIT),
        name="inproj",
    )(x, mod, g_pre, win_b, b_gate, cos_t, sin_t, ln_g, ln_b, sw_b, sb_b, wbs_b)


def _ctxkv_kernel(ctx_ref, mod_ref, g_ref, wk_ref, wv_ref, k_ref, vt_ref, *, ctx_row):
    h = _prenorm(ctx_ref[0], g_ref, mod_ref, ctx_row, 0).astype(BF16)
    k_ref[0] = jnp.dot(h, wk_ref[...], preferred_element_type=F32).astype(BF16)
    vt_ref[0] = jnp.dot(h, wv_ref[...], preferred_element_type=F32).T.astype(BF16)


def _ctxkv(ctx, mod, g_pre, win_b, ctx_row):
    B, Lc, _ = ctx.shape
    tile = pl.BlockSpec((1, Lc, D), lambda b: (b, 0, 0))
    return pl.pallas_call(
        functools.partial(_ctxkv_kernel, ctx_row=ctx_row),
        grid=(B,),
        in_specs=[tile, pl.BlockSpec(mod.shape, lambda b: (0, 0)), pl.BlockSpec((1, D), lambda b: (0, 0)),
                  pl.BlockSpec((D, K_W), lambda b: (0, Q_W // K_W)),
                  pl.BlockSpec((D, V_W), lambda b: (0, (Q_W + K_W) // V_W))],
        out_specs=[tile, pl.BlockSpec((1, D, Lc), lambda b: (b, 0, 0))],
        out_shape=[jax.ShapeDtypeStruct((B, Lc, D), BF16), jax.ShapeDtypeStruct((B, D, Lc), BF16)],
        compiler_params=pltpu.CompilerParams(dimension_semantics=("arbitrary",), vmem_limit_bytes=VMEM_LIMIT),
        name="ctxkv",
    )(ctx, mod, g_pre, win_b, win_b)


ONES_ROWS = 16
KV_TILE_MAX = 2816


def _attn_kernel(q_ref, k_ref, vt_ref, lam_ref, sgt_ref, o_ref, s_scr, acc_scr, *, tq, nt, tk, nk, post_scale):
    lane = lax.broadcasted_iota(jnp.int32, (tq, HEAD_W), 1)
    ones = jnp.ones((ONES_ROWS, tk), BF16)

    def stacked_q(t):
        q = q_ref[0, t * tq:(t + 1) * tq, :]
        zero = jnp.zeros_like(q)
        return jnp.concatenate([jnp.where(lane < DH, q, zero), jnp.where(lane >= DH, q, zero)], axis=0)

    def scores(seq):
        t, j = divmod(seq, nk)
        kb = k_ref[0, j * tk:(j + 1) * tk, :]
        st = lax.dot_general(kb, stacked_q(t), (((1,), (1,)), ((), ())), preferred_element_type=F32)
        s_scr[seq % 2] = st
        return jnp.max(st, axis=0, keepdims=True)

    blk_max = scores(0)
    for seq in range(nt * nk):
        t, j = divmod(seq, nk)
        nxt_max = scores(seq + 1) if seq + 1 < nt * nk else None
        if j == 0:
            m = jnp.full((1, 2 * tq), NEG_BIG, F32)
        m_new = jnp.maximum(m, blk_max)
        pt = jnp.exp2(s_scr[seq % 2] - m_new).astype(BF16)
        vte = jnp.concatenate([vt_ref[0, :, j * tk:(j + 1) * tk], ones], axis=0)
        pv = jnp.dot(vte, pt, preferred_element_type=F32)
        acc_scr[...] = pv if j == 0 else jnp.exp2(m - m_new) * acc_scr[...] + pv
        m, blk_max = m_new, nxt_max
        if j == nk - 1:
            acc = acc_scr[...]
            o1 = acc[:HEAD_W, :tq] / acc[HEAD_W:HEAD_W + 1, :tq]
            o2 = acc[:HEAD_W, tq:] / acc[HEAD_W:HEAD_W + 1, tq:]
            ot = o1 - lam_ref[0:1, 0:tq] * o2
            on = ot * lax.rsqrt(jnp.mean(ot * ot, axis=0, keepdims=True) + RMS_EPS) * sgt_ref[...] * post_scale
            o_ref[0, t * tq:(t + 1) * tq, :] = on.T.astype(BF16)


def _kv_tile(lk):
    best = 128
    for t in range(128, KV_TILE_MAX + 1, 128):
        if lk % t == 0:
            best = t
    return best


def _attn(q, k_all, vt_all, lam, sg_t, lam_init, tq, nt):
    B, L, _ = q.shape
    Lk = k_all.shape[1]
    tk = _kv_tile(Lk)
    return pl.pallas_call(
        functools.partial(_attn_kernel, tq=tq, nt=nt, tk=tk, nk=Lk // tk, post_scale=1.0 - lam_init),
        grid=(B, HEADS, L // (nt * tq)),
        in_specs=[pl.BlockSpec((1, nt * tq, HEAD_W), lambda b, h, i: (b, i, h)),
                  pl.BlockSpec((1, Lk, HEAD_W), lambda b, h, i: (b, 0, h)),
                  pl.BlockSpec((1, HEAD_W, Lk), lambda b, h, i: (b, h, 0)),
                  pl.BlockSpec(lam.shape, lambda b, h, i: (0, 0)),
                  pl.BlockSpec((HEAD_W, tq), lambda b, h, i: (0, 0))],
        out_specs=pl.BlockSpec((1, nt * tq, HEAD_W), lambda b, h, i: (b, i, h)),
        out_shape=jax.ShapeDtypeStruct((B, L, D), BF16),
        scratch_shapes=[pltpu.VMEM((2, tk, 2 * tq), F32), pltpu.VMEM((HEAD_W + ONES_ROWS, 2 * tq), F32)],
        compiler_params=pltpu.CompilerParams(dimension_semantics=("arbitrary",) * 3, vmem_limit_bytes=VMEM_LIMIT),
        name="attn",
    )(q, k_all, vt_all, lam, sg_t)


def _mixout_kernel(o_ref, ga_ref, t_ref, x_ref, mod_ref, gpost_ref, gpre_ref, wba_ref, wout_ref, xn_ref, h2t_ref):
    b = pl.program_id(0)
    y_attn = jnp.dot(o_ref[0], wba_ref[...], preferred_element_type=F32)
    merged = (ga_ref[0].astype(F32) * y_attn + t_ref[0].astype(F32)).astype(BF16)
    y = jnp.dot(merged, wout_ref[...], preferred_element_type=F32)
    gt1 = mod_ref[pl.ds(b, 1), 2 * D:3 * D]
    x_new = x_ref[0] + gt1 * (_rms(y) * gpost_ref[...])
    xn_ref[0] = x_new
    h2 = _prenorm(x_new, gpre_ref, mod_ref, b, 3 * D)
    h2t_ref[0] = h2.T.astype(BF16)


def _mixout(o, ga, t, x, mod, g_post, g_pre_ffn, wba_b, wout_b, tm):
    B, L, _ = x.shape
    full = lambda shape: pl.BlockSpec(shape, lambda b, i: (0,) * len(shape))
    tile = pl.BlockSpec((1, tm, D), lambda b, i: (b, i, 0))
    return pl.pallas_call(
        _mixout_kernel,
        grid=(B, L // tm),
        in_specs=[tile, tile, tile, tile, full(mod.shape), full((1, D)), full((1, D)), full((D, D)), full((D, D))],
        out_specs=[tile, pl.BlockSpec((1, D, tm), lambda b, i: (b, 0, i))],
        out_shape=[jax.ShapeDtypeStruct((B, L, D), F32), jax.ShapeDtypeStruct((B, D, L), BF16)],
        compiler_params=pltpu.CompilerParams(dimension_semantics=("arbitrary", "arbitrary"),
                                             vmem_limit_bytes=VMEM_LIMIT),
        name="mixout",
    )(o, ga, t, x, mod, g_post, g_pre_ffn, wba_b, wout_b)


PEER_SUB = 4 * PEER_KEYS
PEER_TILE = (64, 128)
PEER_TOKEN_SPLIT = 1


def _sorting_network(n):
    pairs, p = [], 1
    while p < n:
        k = p
        while k >= 1:
            for j in range(k % p, n - k, 2 * k):
                for i in range(min(k, n - j - k)):
                    if (i + j) // (2 * p) == (i + j + k) // (2 * p):
                        pairs.append((i + j, i + j + k))
            k //= 2
        p *= 2
    return pairs


def _col_reduce8(x, op):
    for shift in (4, 2, 1):
        x = op(x, pltpu.roll(x, shift, 0))
    return x


def _slabs(x):
    return [x[8 * k:8 * (k + 1), :] for k in range(x.shape[0] // 8)]


def _top16_desc(s, top_scr, p):
    lists = _slabs(s)
    n = len(lists)
    for i, j in _sorting_network(n):
        lists[i], lists[j] = jnp.maximum(lists[i], lists[j]), jnp.minimum(lists[i], lists[j])
    for r in range(PEER_TOPK):
        m = _col_reduce8(lists[0], jnp.maximum)
        top_scr[p, r:r + 1, :] = m[0:1, :]
        if r + 1 < PEER_TOPK:
            hit = lists[0] == m
            for i in range(n - 1 - r):
                lists[i] = jnp.where(hit, lists[i + 1], lists[i])


def _peer_score(h2t, wq_ref, keys_ref, th_scr, t2_scr, r1_scr, top_scr, tt):
    qt = jnp.dot(wq_ref[...], h2t, preferred_element_type=F32)
    neg_inf = jnp.float32(-jnp.inf)
    rep = lambda p, k: jnp.broadcast_to(top_scr[p, k:k + 1, :], (8, tt))
    for h in range(PEER_HEADS):
        halves = []
        for p in range(2):
            r0 = (2 * h + p) * PEER_KEYS
            qhp = qt[r0:r0 + PEER_KEYS, :]
            qn = qhp * lax.rsqrt(jnp.mean(qhp * qhp, axis=0, keepdims=True) + RMS_EPS)
            s = jnp.dot(keys_ref[p], qn.astype(BF16), preferred_element_type=F32)
            halves.append(_slabs(s))
            _top16_desc(s, top_scr, p)
        s1, s2 = halves
        a1_lo, a1_hi = top_scr[0, 0:8, :], top_scr[0, 8:16, :]
        a2_lo, a2_hi = top_scr[1, 0:8, :], top_scr[1, 8:16, :]
        cands = [rep(0, 0) + a2_lo, rep(0, 0) + a2_hi, a1_hi + rep(1, 0)]
        cands += [rep(0, j) + a2_lo for j in range(1, 8)]
        work = list(cands)
        tau = None
        for r in range(PEER_TOPK):
            m = _col_reduce8(functools.reduce(jnp.maximum, work), jnp.maximum)
            if r + 1 < PEER_TOPK:
                work = [jnp.where(c == m, neg_inf, c) for c in work]
            else:
                tau = m
        cmax = rep(0, 0) + rep(1, 0)
        zsum = _col_reduce8(
            functools.reduce(lambda a, c: a + c, [jnp.where(c >= tau, jnp.exp(c - cmax), 0.0) for c in cands]),
            lambda a, c: a + c)
        theta = [jnp.full((8, tt), jnp.inf, F32)] * len(s1)
        for k in range(PEER_TOPK):
            a2k = rep(1, k)
            passing = jnp.minimum(jnp.where(a1_lo + a2k >= tau, a1_lo, jnp.inf),
                                  jnp.where(a1_hi + a2k >= tau, a1_hi, jnp.inf))
            bound = _col_reduce8(passing, jnp.minimum)
            theta = [jnp.where(s >= bound, a2k, t) for s, t in zip(s1, theta)]
        lz = jnp.log2(zsum)
        a1_max, a2_max = rep(0, 0), rep(1, 0)
        lz = lz + 1.0
        for i in range(len(s1)):
            rows = slice(8 * i, 8 * (i + 1))
            th_scr[h, rows, :] = (theta[i] - a2_max) * LOG2E - lz
            t2_scr[h, rows, :] = (s2[i] - a2_max) * LOG2E - lz
            r1_scr[h, rows, :] = (s1[i] - a1_max) * LOG2E


def _peer_kernel(h2t_ref, xn_ref, mod_ref, gpost_ref, wq_ref, keys_ref, u_ref, vt_ref, out_ref,
                 th_scr, t2_scr, r1_scr, top_scr, acc_scr, a_scr, w_scr, thc_scr, r1c_scr, *, tt, ch):
    b = pl.program_id(0)
    c = pl.program_id(2)

    @pl.when(c == 0)
    def _():
        _peer_score(h2t_ref[0], wq_ref, keys_ref, th_scr, t2_scr, r1_scr, top_scr, tt)
        acc_scr[...] = jnp.zeros(acc_scr.shape, F32)

    tr, tl = PEER_TILE
    nj = PEER_SUB // PEER_KEYS
    nsub = ch // PEER_SUB
    step_rows = pl.ds(pl.multiple_of(c * (ch // PEER_KEYS), 8), ch // PEER_KEYS)
    thc_scr[...] = th_scr[:, step_rows, :]
    r1c_scr[...] = r1_scr[:, step_rows, :]

    tw = tt // PEER_TOKEN_SPLIT
    stages = [(sc, hf) for sc in range(nsub) for hf in range(PEER_TOKEN_SPLIT)]

    def experts(sc):
        return slice(sc * PEER_SUB, (sc + 1) * PEER_SUB)

    def tokens(hf):
        return slice(hf * tw, (hf + 1) * tw)

    def activations(s):
        sc, hf = stages[s]
        x = jnp.dot(u_ref[experts(sc), :], h2t_ref[0, :, tokens(hf)], preferred_element_type=F32)
        a_scr[s % 2] = x * (1.0 + lax.erf(x * INV_SQRT2))

    def mix_out(s):
        sc, hf = stages[s]
        acc_scr[:, tokens(hf)] += jnp.dot(vt_ref[:, experts(sc)], w_scr[s % 2], preferred_element_type=F32)

    activations(0)
    for s, (sc, hf) in enumerate(stages):
        if s + 1 < len(stages):
            activations(s + 1)
        if s > 0:
            mix_out(s - 1)
        for rt in range(PEER_KEYS // tr):
            for lt in range(tw // tl):
                r = slice(rt * tr, (rt + 1) * tr)
                l = slice(hf * tw + lt * tl, hf * tw + (lt + 1) * tl)
                ls = slice(lt * tl, (lt + 1) * tl)
                g = [None] * nj
                for h in range(PEER_HEADS):
                    t2t = t2_scr[h, r, l]
                    for j in range(nj):
                        row = slice(sc * nj + j, sc * nj + j + 1)
                        gate = jnp.where(t2t >= thc_scr[h, row, l], jnp.exp2(t2t + r1c_scr[h, row, l]), 0.0)
                        g[j] = gate if h == 0 else g[j] + gate
                for j in range(nj):
                    rows = slice(j * PEER_KEYS + rt * tr, j * PEER_KEYS + (rt + 1) * tr)
                    w_scr[s % 2, rows, ls] = (a_scr[s % 2, rows, ls] * g[j]).astype(BF16)
    mix_out(len(stages) - 1)

    @pl.when(c == pl.num_programs(2) - 1)
    def _():
        y = acc_scr[...].T
        gt2 = mod_ref[pl.ds(b, 1), 5 * D:6 * D]
        out_ref[0] = xn_ref[0] + gt2 * (_rms(y) * gpost_ref[...])


def _peer(h2t, x_new, mod, g_post, wq_t, keys_b, u_b, vt_b, tt, ch):
    B, _, L = h2t.shape
    full = lambda shape: pl.BlockSpec(shape, lambda b, i, c: (0,) * len(shape))
    tile = pl.BlockSpec((1, tt, D), lambda b, i, c: (b, i, 0))
    per_head = pltpu.VMEM((PEER_HEADS, PEER_KEYS, tt), F32)
    nsub = ch // PEER_SUB
    return pl.pallas_call(
        functools.partial(_peer_kernel, tt=tt, ch=ch),
        grid=(B, L // tt, PEER_EXPERTS // ch),
        in_specs=[pl.BlockSpec((1, D, tt), lambda b, i, c: (b, 0, i)), tile, full(mod.shape), full((1, D)),
                  full(wq_t.shape), full(keys_b.shape),
                  pl.BlockSpec((ch, D), lambda b, i, c: (c, 0)),
                  pl.BlockSpec((D, ch), lambda b, i, c: (0, c))],
        out_specs=tile,
        out_shape=jax.ShapeDtypeStruct((B, L, D), F32),
        scratch_shapes=[per_head, per_head, per_head, pltpu.VMEM((2, PEER_TOPK, tt), F32),
                        pltpu.VMEM((D, tt), F32), pltpu.VMEM((2, PEER_SUB, tt // PEER_TOKEN_SPLIT), F32),
                        pltpu.VMEM((2, PEER_SUB, tt // PEER_TOKEN_SPLIT), BF16),
                        pltpu.VMEM((PEER_HEADS, ch // PEER_KEYS, tt), F32),
                        pltpu.VMEM((PEER_HEADS, ch // PEER_KEYS, tt), F32)],
        compiler_params=pltpu.CompilerParams(dimension_semantics=("arbitrary",) * 3, vmem_limit_bytes=VMEM_LIMIT),
        name="peer",
    )(h2t, x_new, mod, g_post, wq_t, keys_b, u_b, vt_b)


def _rope_tables(L):
    nfreq = DH // 4
    inv = (ROPE_BASE ** (-np.arange(nfreq, dtype=np.float32) / nfreq)).astype(np.float32)
    t = np.arange(L, dtype=np.int32)
    rows = (t // GRID_W).astype(np.float32)[:, None] * inv[None, :]
    cols = (t % GRID_W).astype(np.float32)[:, None] * inv[None, :]
    cos64 = np.concatenate([np.cos(rows), np.cos(rows), np.cos(cols), np.cos(cols)], axis=1)
    sin64 = np.concatenate([-np.sin(rows), np.sin(rows), -np.sin(cols), np.sin(cols)], axis=1)
    return jnp.asarray(np.tile(cos64, (1, 2)), F32), jnp.asarray(np.tile(sin64, (1, 2)), F32)


def kernel(x, c, ctx, c_ctx, w_ada, b_ada, g_pre_mix, g_post_mix, g_pre_ffn, g_post_ffn, w_in, b_gate, lambda_q1, lambda_k1, lambda_q2, lambda_k2, subln_g, sgu_ln_g, sgu_ln_b, sgu_w, sgu_b, w_branch_attn, w_branch_sgu, w_out, peer_w_query, peer_sub_keys, peer_u, peer_v):
    B, L, _ = x.shape
    depth = w_ada.shape[0]
    assert depth == 1 and B <= 7 and L % 512 == 0
    lam_init = 0.8 - 0.6 * math.exp(-0.3 * 0)
    tm = 512 if L % 512 == 0 else 256

    row = lambda a: a[0].reshape(1, -1)
    cc = jnp.concatenate([c, c_ctx[None, :], jnp.zeros((8 - B - 1, D), F32)], axis=0)
    mod, lam = _ada(cc, w_ada[0], row(b_ada), row(lambda_q1), row(lambda_k1), row(lambda_q2), row(lambda_k2), lam_init)

    win_b = w_in[0].astype(BF16)
    cos_t, sin_t = _rope_tables(L)
    sb_b = jnp.broadcast_to(sgu_b[0][:, :, None], (SGU_GROUPS, SGU_CHUNK, SGU_W // SGU_GROUPS))
    q, k, vt, t_sgu, g_attn = _inproj(x, mod, row(g_pre_mix), win_b, row(b_gate), cos_t, sin_t,
                                      row(sgu_ln_g), row(sgu_ln_b), sgu_w[0].astype(BF16), sb_b,
                                      w_branch_sgu[0].astype(BF16), tm)
    kc, vct = _ctxkv(ctx, mod, row(g_pre_mix), win_b, B)
    k_all = jnp.concatenate([kc, k], axis=1)
    vt_all = jnp.concatenate([vct, vt], axis=2)
    tq = 256
    sg_t = jnp.broadcast_to(subln_g[0][:, None], (HEAD_W, tq))
    o = _attn(q, k_all, vt_all, lam, sg_t, lam_init, tq, 4 if L % (4 * tq) == 0 else 2)

    x_new, h2t = _mixout(o, g_attn, t_sgu, x, mod, row(g_post_mix), row(g_pre_ffn),
                         w_branch_attn[0].astype(BF16), w_out[0].astype(BF16), tm)
    return _peer(h2t, x_new, mod, row(g_post_ffn), peer_w_query[0].T.astype(BF16),
                 peer_sub_keys[0].astype(BF16), peer_u[0].astype(BF16), peer_v[0].T.astype(BF16), 512, 2048)
```

```python
import functools
import math

import jax
import jax.numpy as jnp
import numpy as np
from jax import lax
from jax.experimental import pallas as pl
from jax.experimental.pallas import tpu as pltpu

F32 = jnp.float32
BF16 = jnp.bfloat16

D = 1024
N_ADA = 6
GRID_W = 64
HEADS = 8
DH = 64
HEAD_W = 2 * DH
ROPE_BASE = 10000.0
SGU_CHUNK = 128
SGU_GROUPS = 8
SGU_W = 1024
Q_W = K_W = V_W = 1024
Z_W = 2 * SGU_W
Z_OFF = Q_W + K_W + V_W
G_OFF = Z_OFF + Z_W
PEER_HEADS = 8
PEER_KEYS = 128
PEER_TOPK = 16
PEER_EXPERTS = PEER_KEYS * PEER_KEYS
RMS_EPS = 1e-6
LN_EPS = 1e-5
LOG2E = 1.4426950408889634
INV_SQRT2 = 0.7071067811865476
NEG_BIG = -1e30
LAM_W = 512

VMEM_LIMIT = 56 * 1024 * 1024


def _gelu(x):
    return 0.5 * x * (1.0 + lax.erf(x * INV_SQRT2))


def _rms(x, eps=RMS_EPS):
    return x * lax.rsqrt(jnp.mean(x * x, axis=-1, keepdims=True) + eps)


def _ada_kernel(cc_ref, w_ref, b_ref, lq1_ref, lk1_ref, lq2_ref, lk2_ref, mod_ref, lam_ref, *, lam_init):
    a = cc_ref[...]
    s = (a * jax.nn.sigmoid(a)).astype(BF16)
    mod_ref[...] = jnp.dot(s, w_ref[...].astype(BF16), preferred_element_type=F32) + b_ref[...]
    d1 = jnp.sum(lq1_ref[...] * lk1_ref[...], axis=-1, keepdims=True)
    d2 = jnp.sum(lq2_ref[...] * lk2_ref[...], axis=-1, keepdims=True)
    lam = jnp.exp(d1) - jnp.exp(d2) + lam_init
    lam_ref[...] = jnp.broadcast_to(lam, lam_ref.shape)


def _ada(cc, w_ada, b_ada, lq1, lk1, lq2, lk2, lam_init):
    tn = 1536
    n = w_ada.shape[1]
    vec = pl.BlockSpec((1, DH), lambda j: (0, 0))
    return pl.pallas_call(
        functools.partial(_ada_kernel, lam_init=lam_init),
        grid=(n // tn,),
        in_specs=[pl.BlockSpec((8, D), lambda j: (0, 0)),
                  pl.BlockSpec((D, tn), lambda j: (0, j)),
                  pl.BlockSpec((1, tn), lambda j: (0, j)),
                  vec, vec, vec, vec],
        out_specs=[pl.BlockSpec((8, tn), lambda j: (0, j)),
                   pl.BlockSpec((8, LAM_W), lambda j: (0, 0))],
        out_shape=[jax.ShapeDtypeStruct((8, n), F32), jax.ShapeDtypeStruct((8, LAM_W), F32)],
        compiler_params=pltpu.CompilerParams(dimension_semantics=("arbitrary",), vmem_limit_bytes=VMEM_LIMIT),
        name="ada",
    )(cc, w_ada, b_ada, lq1, lk1, lq2, lk2)


def _prenorm(x, g_ref, mod_ref, row, col0):
    sh = mod_ref[pl.ds(row, 1), col0:col0 + D]
    sc = mod_ref[pl.ds(row, 1), col0 + D:col0 + 2 * D]
    return _rms(x) * g_ref[...] * (1.0 + sc) + sh


def _inproj_kernel(x_ref, mod_ref, g_ref, win_ref, bg_ref, cos_ref, sin_ref, lng_ref, lnb_ref,
                   sw_ref, sb_ref, wbs_ref, q_ref, k_ref, vt_ref, t_ref, ga_ref, s_scr, *, tm):
    b = pl.program_id(0)
    h = _prenorm(x_ref[0], g_ref, mod_ref, b, 0).astype(BF16)

    lane = lax.broadcasted_iota(jnp.int32, (tm, 128), 1)
    first = (lane & 16) == 0
    cosv = cos_ref[...]
    sinv = sin_ref[...]

    def rope(t):
        swapped = jnp.where(first, pltpu.roll(t, 112, 1), pltpu.roll(t, 16, 1))
        return t * cosv + swapped * sinv

    q = jnp.dot(h, win_ref[:, 0:Q_W], preferred_element_type=F32)
    for j in range(Q_W // 128):
        q_ref[0, :, j * 128:(j + 1) * 128] = (rope(q[:, j * 128:(j + 1) * 128]) * (DH ** -0.5 * LOG2E)).astype(BF16)
    k = jnp.dot(h, win_ref[:, Q_W:Q_W + K_W], preferred_element_type=F32)
    for j in range(K_W // 128):
        k_ref[0, :, j * 128:(j + 1) * 128] = rope(k[:, j * 128:(j + 1) * 128]).astype(BF16)
    vt_ref[0] = jnp.dot(h, win_ref[:, Q_W + K_W:Z_OFF], preferred_element_type=F32).T.astype(BF16)

    z = _gelu(jnp.dot(h, win_ref[:, Z_OFF:G_OFF], preferred_element_type=F32))
    u = z[:, :SGU_W]
    vv = z[:, SGU_W:]
    mu = jnp.mean(vv, axis=-1, keepdims=True)
    xc = vv - mu
    vln = (xc * lax.rsqrt(jnp.mean(xc * xc, axis=-1, keepdims=True) + LN_EPS) * lng_ref[...] + lnb_ref[...]).astype(BF16)
    nc = tm // SGU_CHUNK
    gw = SGU_W // SGU_GROUPS
    for g in range(SGU_GROUPS):
        rhs = jnp.concatenate([vln[n * SGU_CHUNK:(n + 1) * SGU_CHUNK, g * gw:(g + 1) * gw] for n in range(nc)], axis=1)
        mixed = jnp.dot(sw_ref[g], rhs, preferred_element_type=F32)
        for n in range(nc):
            rows = slice(n * SGU_CHUNK, (n + 1) * SGU_CHUNK)
            cols = slice(g * gw, (g + 1) * gw)
            s_scr[rows, cols] = (u[rows, cols] * (mixed[:, n * gw:(n + 1) * gw] + sb_ref[g])).astype(BF16)
    y_sgu = jnp.dot(s_scr[...], wbs_ref[...], preferred_element_type=F32)

    gate = jax.nn.sigmoid(jnp.dot(h, win_ref[:, G_OFF:G_OFF + 2 * D], preferred_element_type=F32) + bg_ref[...])
    ga_ref[0] = gate[:, :D].astype(BF16)
    t_ref[0] = (gate[:, D:] * y_sgu).astype(BF16)


def _inproj(x, mod, g_pre, win_b, b_gate, cos_t, sin_t, ln_g, ln_b, sw_b, sb_b, wbs_b, tm):
    B, L, _ = x.shape
    full = lambda shape: pl.BlockSpec(shape, lambda b, i: (0,) * len(shape))
    tile = pl.BlockSpec((1, tm, D), lambda b, i: (b, i, 0))
    tile_t = pl.BlockSpec((1, D, tm), lambda b, i: (b, 0, i))
    out = jax.ShapeDtypeStruct((B, L, D), BF16)
    out_t = jax.ShapeDtypeStruct((B, D, L), BF16)
    return pl.pallas_call(
        functools.partial(_inproj_kernel, tm=tm),
        grid=(B, L // tm),
        in_specs=[tile, full(mod.shape), full((1, D)), full(win_b.shape), full((1, 2 * D)),
                  pl.BlockSpec((tm, 128), lambda b, i: (i, 0)), pl.BlockSpec((tm, 128), lambda b, i: (i, 0)),
                  full((1, SGU_W)), full((1, SGU_W)), full(sw_b.shape), full(sb_b.shape), full(wbs_b.shape)],
        out_specs=[tile, tile, tile_t, tile, tile],
        out_shape=[out, out, out_t, out, out],
        scratch_shapes=[pltpu.VMEM((tm, SGU_W), BF16)],
        compiler_params=pltpu.CompilerParams(dimension_semantics=("arbitrary", "arbitrary"),
                                             vmem_limit_bytes=VMEM_LIMIT),
        name="inproj",
    )(x, mod, g_pre, win_b, b_gate, cos_t, sin_t, ln_g, ln_b, sw_b, sb_b, wbs_b)


def _ctxkv_kernel(ctx_ref, mod_ref, g_ref, wk_ref, wv_ref, k_ref, vt_ref, *, ctx_row):
    h = _prenorm(ctx_ref[0], g_ref, mod_ref, ctx_row, 0).astype(BF16)
    k_ref[0] = jnp.dot(h, wk_ref[...], preferred_element_type=F32).astype(BF16)
    vt_ref[0] = jnp.dot(h, wv_ref[...], preferred_element_type=F32).T.astype(BF16)


def _ctxkv(ctx, mod, g_pre, win_b, ctx_row):
    B, Lc, _ = ctx.shape
    tile = pl.BlockSpec((1, Lc, D), lambda b: (b, 0, 0))
    return pl.pallas_call(
        functools.partial(_ctxkv_kernel, ctx_row=ctx_row),
        grid=(B,),
        in_specs=[tile, pl.BlockSpec(mod.shape, lambda b: (0, 0)), pl.BlockSpec((1, D), lambda b: (0, 0)),
                  pl.BlockSpec((D, K_W), lambda b: (0, Q_W // K_W)),
                  pl.BlockSpec((D, V_W), lambda b: (0, (Q_W + K_W) // V_W))],
        out_specs=[tile, pl.BlockSpec((1, D, Lc), lambda b: (b, 0, 0))],
        out_shape=[jax.ShapeDtypeStruct((B, Lc, D), BF16), jax.ShapeDtypeStruct((B, D, Lc), BF16)],
        compiler_params=pltpu.CompilerParams(dimension_semantics=("arbitrary",), vmem_limit_bytes=VMEM_LIMIT),
        name="ctxkv",
    )(ctx, mod, g_pre, win_b, win_b)


ONES_ROWS = 16
KV_TILE_MAX = 4224


def _attn_kernel(q_ref, k_ref, vt_ref, lam_ref, sgt_ref, o_ref, s_scr, acc_scr, *, tq, nt, tk, nk, post_scale):
    lane = lax.broadcasted_iota(jnp.int32, (tq, HEAD_W), 1)
    ones = jnp.ones((ONES_ROWS, tk), BF16)

    def stacked_q(t):
        q = q_ref[0, t * tq:(t + 1) * tq, :]
        zero = jnp.zeros_like(q)
        return jnp.concatenate([jnp.where(lane < DH, q, zero), jnp.where(lane >= DH, q, zero)], axis=0)

    def scores(seq):
        t, j = divmod(seq, nk)
        kb = k_ref[0, j * tk:(j + 1) * tk, :]
        st = lax.dot_general(kb, stacked_q(t), (((1,), (1,)), ((), ())), preferred_element_type=F32)
        s_scr[seq % 2] = st
        return jnp.max(st, axis=0, keepdims=True)

    blk_max = scores(0)
    for seq in range(nt * nk):
        t, j = divmod(seq, nk)
        nxt_max = scores(seq + 1) if seq + 1 < nt * nk else None
        if j == 0:
            m = jnp.full((1, 2 * tq), NEG_BIG, F32)
        m_new = jnp.maximum(m, blk_max)
        pt = jnp.exp2(s_scr[seq % 2] - m_new).astype(BF16)
        vte = jnp.concatenate([vt_ref[0, :, j * tk:(j + 1) * tk], ones], axis=0)
        pv = jnp.dot(vte, pt, preferred_element_type=F32)
        acc_scr[...] = pv if j == 0 else jnp.exp2(m - m_new) * acc_scr[...] + pv
        m, blk_max = m_new, nxt_max
        if j == nk - 1:
            acc = acc_scr[...]
            o1 = acc[:HEAD_W, :tq] / acc[HEAD_W:HEAD_W + 1, :tq]
            o2 = acc[:HEAD_W, tq:] / acc[HEAD_W:HEAD_W + 1, tq:]
            ot = o1 - lam_ref[0:1, 0:tq] * o2
            on = ot * lax.rsqrt(jnp.mean(ot * ot, axis=0, keepdims=True) + RMS_EPS) * sgt_ref[...] * post_scale
            o_ref[0, t * tq:(t + 1) * tq, :] = on.T.astype(BF16)


def _kv_tile(lk):
    best = 128
    for t in range(128, KV_TILE_MAX + 1, 128):
        if lk % t == 0:
            best = t
    return best


def _attn(q, k_all, vt_all, lam, sg_t, lam_init, tq, nt):
    B, L, _ = q.shape
    Lk = k_all.shape[1]
    tk = _kv_tile(Lk)
    return pl.pallas_call(
        functools.partial(_attn_kernel, tq=tq, nt=nt, tk=tk, nk=Lk // tk, post_scale=1.0 - lam_init),
        grid=(B, HEADS, L // (nt * tq)),
        in_specs=[pl.BlockSpec((1, nt * tq, HEAD_W), lambda b, h, i: (b, i, h)),
                  pl.BlockSpec((1, Lk, HEAD_W), lambda b, h, i: (b, 0, h)),
                  pl.BlockSpec((1, HEAD_W, Lk), lambda b, h, i: (b, h, 0)),
                  pl.BlockSpec(lam.shape, lambda b, h, i: (0, 0)),
                  pl.BlockSpec((HEAD_W, tq), lambda b, h, i: (0, 0))],
        out_specs=pl.BlockSpec((1, nt * tq, HEAD_W), lambda b, h, i: (b, i, h)),
        out_shape=jax.ShapeDtypeStruct((B, L, D), BF16),
        scratch_shapes=[pltpu.VMEM((2, tk, 2 * tq), F32), pltpu.VMEM((HEAD_W + ONES_ROWS, 2 * tq), F32)],
        compiler_params=pltpu.CompilerParams(dimension_semantics=("arbitrary",) * 3, vmem_limit_bytes=VMEM_LIMIT),
        name="attn",
    )(q, k_all, vt_all, lam, sg_t)


def _mixout_kernel(o_ref, ga_ref, t_ref, x_ref, mod_ref, gpost_ref, gpre_ref, wba_ref, wout_ref, xn_ref, h2t_ref):
    b = pl.program_id(0)
    y_attn = jnp.dot(o_ref[0], wba_ref[...], preferred_element_type=F32)
    merged = (ga_ref[0].astype(F32) * y_attn + t_ref[0].astype(F32)).astype(BF16)
    y = jnp.dot(merged, wout_ref[...], preferred_element_type=F32)
    gt1 = mod_ref[pl.ds(b, 1), 2 * D:3 * D]
    x_new = x_ref[0] + gt1 * (_rms(y) * gpost_ref[...])
    xn_ref[0] = x_new
    h2 = _prenorm(x_new, gpre_ref, mod_ref, b, 3 * D)
    h2t_ref[0] = h2.T.astype(BF16)


def _mixout(o, ga, t, x, mod, g_post, g_pre_ffn, wba_b, wout_b, tm):
    B, L, _ = x.shape
    full = lambda shape: pl.BlockSpec(shape, lambda b, i: (0,) * len(shape))
    tile = pl.BlockSpec((1, tm, D), lambda b, i: (b, i, 0))
    return pl.pallas_call(
        _mixout_kernel,
        grid=(B, L // tm),
        in_specs=[tile, tile, tile, tile, full(mod.shape), full((1, D)), full((1, D)), full((D, D)), full((D, D))],
        out_specs=[tile, pl.BlockSpec((1, D, tm), lambda b, i: (b, 0, i))],
        out_shape=[jax.ShapeDtypeStruct((B, L, D), F32), jax.ShapeDtypeStruct((B, D, L), BF16)],
        compiler_params=pltpu.CompilerParams(dimension_semantics=("arbitrary", "arbitrary"),
                                             vmem_limit_bytes=VMEM_LIMIT),
        name="mixout",
    )(o, ga, t, x, mod, g_post, g_pre_ffn, wba_b, wout_b)


PEER_SUB = 4 * PEER_KEYS
PEER_TILE = (16, 512)
PEER_TOKEN_SPLIT = 1


def _sorting_network(n):
    pairs, p = [], 1
    while p < n:
        k = p
        while k >= 1:
            for j in range(k % p, n - k, 2 * k):
                for i in range(min(k, n - j - k)):
                    if (i + j) // (2 * p) == (i + j + k) // (2 * p):
                        pairs.append((i + j, i + j + k))
            k //= 2
        p *= 2
    return pairs


def _col_reduce8(x, op):
    for shift in (4, 2, 1):
        x = op(x, pltpu.roll(x, shift, 0))
    return x


def _slabs(x):
    return [x[8 * k:8 * (k + 1), :] for k in range(x.shape[0] // 8)]


def _top16_desc(s, top_scr, p):
    lists = _slabs(s)
    n = len(lists)
    for i, j in _sorting_network(n):
        lists[i], lists[j] = jnp.maximum(lists[i], lists[j]), jnp.minimum(lists[i], lists[j])
    for r in range(PEER_TOPK):
        m = _col_reduce8(lists[0], jnp.maximum)
        top_scr[p, r:r + 1, :] = m[0:1, :]
        if r + 1 < PEER_TOPK:
            hit = lists[0] == m
            for i in range(n - 1 - r):
                lists[i] = jnp.where(hit, lists[i + 1], lists[i])


def _peer_score(h2t, wq_ref, keys_ref, th_scr, t2_scr, r1_scr, top_scr, tt):
    qt = jnp.dot(wq_ref[...], h2t, preferred_element_type=F32)
    neg_inf = jnp.float32(-jnp.inf)
    rep = lambda p, k: jnp.broadcast_to(top_scr[p, k:k + 1, :], (8, tt))
    for h in range(PEER_HEADS):
        halves = []
        for p in range(2):
            r0 = (2 * h + p) * PEER_KEYS
            qhp = qt[r0:r0 + PEER_KEYS, :]
            qn = qhp * lax.rsqrt(jnp.mean(qhp * qhp, axis=0, keepdims=True) + RMS_EPS)
            s = jnp.dot(keys_ref[p], qn.astype(BF16), preferred_element_type=F32)
            halves.append(_slabs(s))
            _top16_desc(s, top_scr, p)
        s1, s2 = halves
        a1_lo, a1_hi = top_scr[0, 0:8, :], top_scr[0, 8:16, :]
        a2_lo, a2_hi = top_scr[1, 0:8, :], top_scr[1, 8:16, :]
        cands = [rep(0, 0) + a2_lo, rep(0, 0) + a2_hi, a1_hi + rep(1, 0)]
        cands += [rep(0, j) + a2_lo for j in range(1, 8)]
        work = list(cands)
        tau = None
        for r in range(PEER_TOPK):
            m = _col_reduce8(functools.reduce(jnp.maximum, work), jnp.maximum)
            if r + 1 < PEER_TOPK:
                work = [jnp.where(c == m, neg_inf, c) for c in work]
            else:
                tau = m
        cmax = rep(0, 0) + rep(1, 0)
        zsum = _col_reduce8(
            functools.reduce(lambda a, c: a + c, [jnp.where(c >= tau, jnp.exp(c - cmax), 0.0) for c in cands]),
            lambda a, c: a + c)
        theta = [jnp.full((8, tt), jnp.inf, F32)] * len(s1)
        for k in range(PEER_TOPK):
            a2k = rep(1, k)
            passing = jnp.minimum(jnp.where(a1_lo + a2k >= tau, a1_lo, jnp.inf),
                                  jnp.where(a1_hi + a2k >= tau, a1_hi, jnp.inf))
            bound = _col_reduce8(passing, jnp.minimum)
            theta = [jnp.where(s >= bound, a2k, t) for s, t in zip(s1, theta)]
        lz = jnp.log2(zsum)
        a1_max, a2_max = rep(0, 0), rep(1, 0)
        lz = lz + 1.0
        for i in range(len(s1)):
            rows = slice(8 * i, 8 * (i + 1))
            th_scr[h, rows, :] = (theta[i] - a2_max) * LOG2E - lz
            t2_scr[h, rows, :] = (s2[i] - a2_max) * LOG2E - lz
            r1_scr[h, rows, :] = (s1[i] - a1_max) * LOG2E


def _peer_kernel(h2t_ref, xn_ref, mod_ref, gpost_ref, wq_ref, keys_ref, u_ref, vt_ref, out_ref,
                 th_scr, t2_scr, r1_scr, top_scr, acc_scr, a_scr, w_scr, thc_scr, r1c_scr, *, tt, ch):
    b = pl.program_id(0)
    c = pl.program_id(2)

    @pl.when(c == 0)
    def _():
        _peer_score(h2t_ref[0], wq_ref, keys_ref, th_scr, t2_scr, r1_scr, top_scr, tt)
        acc_scr[...] = jnp.zeros(acc_scr.shape, F32)

    tr, tl = PEER_TILE
    nj = PEER_SUB // PEER_KEYS
    nsub = ch // PEER_SUB
    step_rows = pl.ds(pl.multiple_of(c * (ch // PEER_KEYS), 8), ch // PEER_KEYS)
    thc_scr[...] = th_scr[:, step_rows, :]
    r1c_scr[...] = r1_scr[:, step_rows, :]

    tw = tt // PEER_TOKEN_SPLIT
    stages = [(sc, hf) for sc in range(nsub) for hf in range(PEER_TOKEN_SPLIT)]

    def experts(sc):
        return slice(sc * PEER_SUB, (sc + 1) * PEER_SUB)

    def tokens(hf):
        return slice(hf * tw, (hf + 1) * tw)

    def activations(s):
        sc, hf = stages[s]
        x = jnp.dot(u_ref[experts(sc), :], h2t_ref[0, :, tokens(hf)], preferred_element_type=F32)
        a_scr[s % 2] = x * (1.0 + lax.erf(x * INV_SQRT2))

    def mix_out(s):
        sc, hf = stages[s]
        acc_scr[:, tokens(hf)] += jnp.dot(vt_ref[:, experts(sc)], w_scr[s % 2], preferred_element_type=F32)

    activations(0)
    for s, (sc, hf) in enumerate(stages):
        if s + 1 < len(stages):
            activations(s + 1)
        if s > 0:
            mix_out(s - 1)
        for rt in range(PEER_KEYS // tr):
            for lt in range(tw // tl):
                r = slice(rt * tr, (rt + 1) * tr)
                l = slice(hf * tw + lt * tl, hf * tw + (lt + 1) * tl)
                ls = slice(lt * tl, (lt + 1) * tl)
                g = [None] * nj
                for h in range(PEER_HEADS):
                    t2t = t2_scr[h, r, l]
                    for j in range(nj):
                        row = slice(sc * nj + j, sc * nj + j + 1)
                        gate = jnp.where(t2t >= thc_scr[h, row, l], jnp.exp2(t2t + r1c_scr[h, row, l]), 0.0)
                        g[j] = gate if h == 0 else g[j] + gate
                for j in range(nj):
                    rows = slice(j * PEER_KEYS + rt * tr, j * PEER_KEYS + (rt + 1) * tr)
                    w_scr[s % 2, rows, ls] = (a_scr[s % 2, rows, ls] * g[j]).astype(BF16)
    mix_out(len(stages) - 1)

    @pl.when(c == pl.num_programs(2) - 1)
    def _():
        y = acc_scr[...].T
        gt2 = mod_ref[pl.ds(b, 1), 5 * D:6 * D]
        out_ref[0] = xn_ref[0] + gt2 * (_rms(y) * gpost_ref[...])


def _peer(h2t, x_new, mod, g_post, wq_t, keys_b, u_b, vt_b, tt, ch):
    B, _, L = h2t.shape
    full = lambda shape: pl.BlockSpec(shape, lambda b, i, c: (0,) * len(shape))
    tile = pl.BlockSpec((1, tt, D), lambda b, i, c: (b, i, 0))
    per_head = pltpu.VMEM((PEER_HEADS, PEER_KEYS, tt), F32)
    nsub = ch // PEER_SUB
    return pl.pallas_call(
        functools.partial(_peer_kernel, tt=tt, ch=ch),
        grid=(B, L // tt, PEER_EXPERTS // ch),
        in_specs=[pl.BlockSpec((1, D, tt), lambda b, i, c: (b, 0, i)), tile, full(mod.shape), full((1, D)),
                  full(wq_t.shape), full(keys_b.shape),
                  pl.BlockSpec((ch, D), lambda b, i, c: (c, 0)),
                  pl.BlockSpec((D, ch), lambda b, i, c: (0, c))],
        out_specs=tile,
        out_shape=jax.ShapeDtypeStruct((B, L, D), F32),
        scratch_shapes=[per_head, per_head, per_head, pltpu.VMEM((2, PEER_TOPK, tt), F32),
                        pltpu.VMEM((D, tt), F32), pltpu.VMEM((2, PEER_SUB, tt // PEER_TOKEN_SPLIT), F32),
                        pltpu.VMEM((2, PEER_SUB, tt // PEER_TOKEN_SPLIT), BF16),
                        pltpu.VMEM((PEER_HEADS, ch // PEER_KEYS, tt), F32),
                        pltpu.VMEM((PEER_HEADS, ch // PEER_KEYS, tt), F32)],
        compiler_params=pltpu.CompilerParams(dimension_semantics=("arbitrary",) * 3, vmem_limit_bytes=VMEM_LIMIT),
        name="peer",
    )(h2t, x_new, mod, g_post, wq_t, keys_b, u_b, vt_b)


def _rope_tables(L):
    nfreq = DH // 4
    inv = (ROPE_BASE ** (-np.arange(nfreq, dtype=np.float32) / nfreq)).astype(np.float32)
    t = np.arange(L, dtype=np.int32)
    rows = (t // GRID_W).astype(np.float32)[:, None] * inv[None, :]
    cols = (t % GRID_W).astype(np.float32)[:, None] * inv[None, :]
    cos64 = np.concatenate([np.cos(rows), np.cos(rows), np.cos(cols), np.cos(cols)], axis=1)
    sin64 = np.concatenate([-np.sin(rows), np.sin(rows), -np.sin(cols), np.sin(cols)], axis=1)
    return jnp.asarray(np.tile(cos64, (1, 2)), F32), jnp.asarray(np.tile(sin64, (1, 2)), F32)


def kernel(x, c, ctx, c_ctx, w_ada, b_ada, g_pre_mix, g_post_mix, g_pre_ffn, g_post_ffn, w_in, b_gate, lambda_q1, lambda_k1, lambda_q2, lambda_k2, subln_g, sgu_ln_g, sgu_ln_b, sgu_w, sgu_b, w_branch_attn, w_branch_sgu, w_out, peer_w_query, peer_sub_keys, peer_u, peer_v):
    B, L, _ = x.shape
    depth = w_ada.shape[0]
    assert depth == 1 and B <= 7 and L % 512 == 0
    lam_init = 0.8 - 0.6 * math.exp(-0.3 * 0)
    tm = 512 if L % 512 == 0 else 256

    row = lambda a: a[0].reshape(1, -1)
    cc = jnp.concatenate([c, c_ctx[None, :], jnp.zeros((8 - B - 1, D), F32)], axis=0)
    mod, lam = _ada(cc, w_ada[0], row(b_ada), row(lambda_q1), row(lambda_k1), row(lambda_q2), row(lambda_k2), lam_init)

    win_b = w_in[0].astype(BF16)
    cos_t, sin_t = _rope_tables(L)
    sb_b = jnp.broadcast_to(sgu_b[0][:, :, None], (SGU_GROUPS, SGU_CHUNK, SGU_W // SGU_GROUPS))
    q, k, vt, t_sgu, g_attn = _inproj(x, mod, row(g_pre_mix), win_b, row(b_gate), cos_t, sin_t,
                                      row(sgu_ln_g), row(sgu_ln_b), sgu_w[0].astype(BF16), sb_b,
                                      w_branch_sgu[0].astype(BF16), tm)
    kc, vct = _ctxkv(ctx, mod, row(g_pre_mix), win_b, B)
    k_all = jnp.concatenate([kc, k], axis=1)
    vt_all = jnp.concatenate([vct, vt], axis=2)
    tq = 256
    sg_t = jnp.broadcast_to(subln_g[0][:, None], (HEAD_W, tq))
    o = _attn(q, k_all, vt_all, lam, sg_t, lam_init, tq, 4 if L % (4 * tq) == 0 else 2)

    x_new, h2t = _mixout(o, g_attn, t_sgu, x, mod, row(g_post_mix), row(g_pre_ffn),
                         w_branch_attn[0].astype(BF16), w_out[0].astype(BF16), tm)
    return _peer(h2t, x_new, mod, row(g_post_ffn), peer_w_query[0].T.astype(BF16),
                 peer_sub_keys[0].astype(BF16), peer_u[0].astype(BF16), peer_v[0].T.astype(BF16), 512, 2048)
```

```python
import functools
import math

import jax
import jax.numpy as jnp
import numpy as np
from jax import lax
from jax.experimental import pallas as pl
from jax.experimental.pallas import tpu as pltpu

F32 = jnp.float32
BF16 = jnp.bfloat16

D = 1024
N_ADA = 6
GRID_W = 64
HEADS = 8
DH = 64
HEAD_W = 2 * DH
ROPE_BASE = 10000.0
SGU_CHUNK = 128
SGU_GROUPS = 8
SGU_W = 1024
Q_W = K_W = V_W = 1024
Z_W = 2 * SGU_W
Z_OFF = Q_W + K_W + V_W
G_OFF = Z_OFF + Z_W
PEER_HEADS = 8
PEER_KEYS = 128
PEER_TOPK = 16
PEER_EXPERTS = PEER_KEYS * PEER_KEYS
RMS_EPS = 1e-6
LN_EPS = 1e-5
LOG2E = 1.4426950408889634
INV_SQRT2 = 0.7071067811865476
NEG_BIG = -1e30
LAM_W = 512

VMEM_LIMIT = 56 * 1024 * 1024
PEER_VMEM_LIMIT = 61 * 1024 * 1024


def _gelu(x):
    return 0.5 * x * (1.0 + lax.erf(x * INV_SQRT2))


def _rms(x, eps=RMS_EPS):
    return x * lax.rsqrt(jnp.mean(x * x, axis=-1, keepdims=True) + eps)


def _ada_kernel(cc_ref, w_ref, b_ref, lq1_ref, lk1_ref, lq2_ref, lk2_ref, mod_ref, lam_ref, *, lam_init):
    a = cc_ref[...]
    s = (a * jax.nn.sigmoid(a)).astype(BF16)
    mod_ref[...] = jnp.dot(s, w_ref[...].astype(BF16), preferred_element_type=F32) + b_ref[...]
    d1 = jnp.sum(lq1_ref[...] * lk1_ref[...], axis=-1, keepdims=True)
    d2 = jnp.sum(lq2_ref[...] * lk2_ref[...], axis=-1, keepdims=True)
    lam = jnp.exp(d1) - jnp.exp(d2) + lam_init
    lam_ref[...] = jnp.broadcast_to(lam, lam_ref.shape)


def _ada(cc, w_ada, b_ada, lq1, lk1, lq2, lk2, lam_init):
    tn = 1536
    n = w_ada.shape[1]
    vec = pl.BlockSpec((1, DH), lambda j: (0, 0))
    return pl.pallas_call(
        functools.partial(_ada_kernel, lam_init=lam_init),
        grid=(n // tn,),
        in_specs=[pl.BlockSpec((8, D), lambda j: (0, 0)),
                  pl.BlockSpec((D, tn), lambda j: (0, j)),
                  pl.BlockSpec((1, tn), lambda j: (0, j)),
                  vec, vec, vec, vec],
        out_specs=[pl.BlockSpec((8, tn), lambda j: (0, j)),
                   pl.BlockSpec((8, LAM_W), lambda j: (0, 0))],
        out_shape=[jax.ShapeDtypeStruct((8, n), F32), jax.ShapeDtypeStruct((8, LAM_W), F32)],
        compiler_params=pltpu.CompilerParams(dimension_semantics=("arbitrary",), vmem_limit_bytes=VMEM_LIMIT),
        name="ada",
    )(cc, w_ada, b_ada, lq1, lk1, lq2, lk2)


def _prenorm(x, g_ref, mod_ref, row, col0):
    sh = mod_ref[pl.ds(row, 1), col0:col0 + D]
    sc = mod_ref[pl.ds(row, 1), col0 + D:col0 + 2 * D]
    return _rms(x) * g_ref[...] * (1.0 + sc) + sh


def _inproj_kernel(x_ref, mod_ref, g_ref, win_ref, bg_ref, cos_ref, sin_ref, lng_ref, lnb_ref,
                   sw_ref, sb_ref, wbs_ref, q_ref, k_ref, vt_ref, t_ref, ga_ref, s_scr, *, tm):
    b = pl.program_id(0)
    h = _prenorm(x_ref[0], g_ref, mod_ref, b, 0).astype(BF16)

    lane = lax.broadcasted_iota(jnp.int32, (tm, 128), 1)
    first = (lane & 16) == 0
    cosv = cos_ref[...]
    sinv = sin_ref[...]

    def rope(t):
        swapped = jnp.where(first, pltpu.roll(t, 112, 1), pltpu.roll(t, 16, 1))
        return t * cosv + swapped * sinv

    q = jnp.dot(h, win_ref[:, 0:Q_W], preferred_element_type=F32)
    for j in range(Q_W // 128):
        q_ref[0, :, j * 128:(j + 1) * 128] = (rope(q[:, j * 128:(j + 1) * 128]) * (DH ** -0.5 * LOG2E)).astype(BF16)
    k = jnp.dot(h, win_ref[:, Q_W:Q_W + K_W], preferred_element_type=F32)
    for j in range(K_W // 128):
        k_ref[0, :, j * 128:(j + 1) * 128] = rope(k[:, j * 128:(j + 1) * 128]).astype(BF16)
    vt_ref[0] = jnp.dot(h, win_ref[:, Q_W + K_W:Z_OFF], preferred_element_type=F32).T.astype(BF16)

    z = _gelu(jnp.dot(h, win_ref[:, Z_OFF:G_OFF], preferred_element_type=F32))
    u = z[:, :SGU_W]
    vv = z[:, SGU_W:]
    mu = jnp.mean(vv, axis=-1, keepdims=True)
    xc = vv - mu
    vln = (xc * lax.rsqrt(jnp.mean(xc * xc, axis=-1, keepdims=True) + LN_EPS) * lng_ref[...] + lnb_ref[...]).astype(BF16)
    nc = tm // SGU_CHUNK
    gw = SGU_W // SGU_GROUPS
    for g in range(SGU_GROUPS):
        rhs = jnp.concatenate([vln[n * SGU_CHUNK:(n + 1) * SGU_CHUNK, g * gw:(g + 1) * gw] for n in range(nc)], axis=1)
        mixed = jnp.dot(sw_ref[g], rhs, preferred_element_type=F32)
        for n in range(nc):
            rows = slice(n * SGU_CHUNK, (n + 1) * SGU_CHUNK)
            cols = slice(g * gw, (g + 1) * gw)
            s_scr[rows, cols] = (u[rows, cols] * (mixed[:, n * gw:(n + 1) * gw] + sb_ref[g])).astype(BF16)
    y_sgu = jnp.dot(s_scr[...], wbs_ref[...], preferred_element_type=F32)

    gate = jax.nn.sigmoid(jnp.dot(h, win_ref[:, G_OFF:G_OFF + 2 * D], preferred_element_type=F32) + bg_ref[...])
    ga_ref[0] = gate[:, :D].astype(BF16)
    t_ref[0] = (gate[:, D:] * y_sgu).astype(BF16)


def _inproj(x, mod, g_pre, win_b, b_gate, cos_t, sin_t, ln_g, ln_b, sw_b, sb_b, wbs_b, tm):
    B, L, _ = x.shape
    full = lambda shape: pl.BlockSpec(shape, lambda b, i: (0,) * len(shape))
    tile = pl.BlockSpec((1, tm, D), lambda b, i: (b, i, 0))
    tile_t = pl.BlockSpec((1, D, tm), lambda b, i: (b, 0, i))
    out = jax.ShapeDtypeStruct((B, L, D), BF16)
    out_t = jax.ShapeDtypeStruct((B, D, L), BF16)
    return pl.pallas_call(
        functools.partial(_inproj_kernel, tm=tm),
        grid=(B, L // tm),
        in_specs=[tile, full(mod.shape), full((1, D)), full(win_b.shape), full((1, 2 * D)),
                  pl.BlockSpec((tm, 128), lambda b, i: (i, 0)), pl.BlockSpec((tm, 128), lambda b, i: (i, 0)),
                  full((1, SGU_W)), full((1, SGU_W)), full(sw_b.shape), full(sb_b.shape), full(wbs_b.shape)],
        out_specs=[tile, tile, tile_t, tile, tile],
        out_shape=[out, out, out_t, out, out],
        scratch_shapes=[pltpu.VMEM((tm, SGU_W), BF16)],
        compiler_params=pltpu.CompilerParams(dimension_semantics=("arbitrary", "arbitrary"),
                                             vmem_limit_bytes=VMEM_LIMIT),
        name="inproj",
    )(x, mod, g_pre, win_b, b_gate, cos_t, sin_t, ln_g, ln_b, sw_b, sb_b, wbs_b)


def _ctxkv_kernel(ctx_ref, mod_ref, g_ref, wk_ref, wv_ref, k_ref, vt_ref, *, ctx_row):
    h = _prenorm(ctx_ref[0], g_ref, mod_ref, ctx_row, 0).astype(BF16)
    k_ref[0] = jnp.dot(h, wk_ref[...], preferred_element_type=F32).astype(BF16)
    vt_ref[0] = jnp.dot(h, wv_ref[...], preferred_element_type=F32).T.astype(BF16)


def _ctxkv(ctx, mod, g_pre, win_b, ctx_row):
    B, Lc, _ = ctx.shape
    tile = pl.BlockSpec((1, Lc, D), lambda b: (b, 0, 0))
    return pl.pallas_call(
        functools.partial(_ctxkv_kernel, ctx_row=ctx_row),
        grid=(B,),
        in_specs=[tile, pl.BlockSpec(mod.shape, lambda b: (0, 0)), pl.BlockSpec((1, D), lambda b: (0, 0)),
                  pl.BlockSpec((D, K_W), lambda b: (0, Q_W // K_W)),
                  pl.BlockSpec((D, V_W), lambda b: (0, (Q_W + K_W) // V_W))],
        out_specs=[tile, pl.BlockSpec((1, D, Lc), lambda b: (b, 0, 0))],
        out_shape=[jax.ShapeDtypeStruct((B, Lc, D), BF16), jax.ShapeDtypeStruct((B, D, Lc), BF16)],
        compiler_params=pltpu.CompilerParams(dimension_semantics=("arbitrary",), vmem_limit_bytes=VMEM_LIMIT),
        name="ctxkv",
    )(ctx, mod, g_pre, win_b, win_b)


ONES_ROWS = 16
KV_TILE_MAX = 4224


def _attn_kernel(q_ref, k_ref, vt_ref, lam_ref, sgt_ref, o_ref, s_scr, acc_scr, *, tq, nt, tk, nk, post_scale):
    lane = lax.broadcasted_iota(jnp.int32, (tq, HEAD_W), 1)
    ones = jnp.ones((ONES_ROWS, tk), BF16)

    def stacked_q(t):
        q = q_ref[0, t * tq:(t + 1) * tq, :]
        zero = jnp.zeros_like(q)
        return jnp.concatenate([jnp.where(lane < DH, q, zero), jnp.where(lane >= DH, q, zero)], axis=0)

    def scores(seq):
        t, j = divmod(seq, nk)
        kb = k_ref[0, j * tk:(j + 1) * tk, :]
        st = lax.dot_general(kb, stacked_q(t), (((1,), (1,)), ((), ())), preferred_element_type=F32)
        s_scr[seq % 2] = st
        return jnp.max(st, axis=0, keepdims=True)

    blk_max = scores(0)
    for seq in range(nt * nk):
        t, j = divmod(seq, nk)
        nxt_max = scores(seq + 1) if seq + 1 < nt * nk else None
        if j == 0:
            m = jnp.full((1, 2 * tq), NEG_BIG, F32)
        m_new = jnp.maximum(m, blk_max)
        pt = jnp.exp2(s_scr[seq % 2] - m_new).astype(BF16)
        vte = jnp.concatenate([vt_ref[0, :, j * tk:(j + 1) * tk], ones], axis=0)
        pv = jnp.dot(vte, pt, preferred_element_type=F32)
        acc_scr[...] = pv if j == 0 else jnp.exp2(m - m_new) * acc_scr[...] + pv
        m, blk_max = m_new, nxt_max
        if j == nk - 1:
            acc = acc_scr[...]
            o1 = acc[:HEAD_W, :tq] / acc[HEAD_W:HEAD_W + 1, :tq]
            o2 = acc[:HEAD_W, tq:] / acc[HEAD_W:HEAD_W + 1, tq:]
            ot = o1 - lam_ref[0:1, 0:tq] * o2
            on = ot * lax.rsqrt(jnp.mean(ot * ot, axis=0, keepdims=True) + RMS_EPS) * sgt_ref[...] * post_scale
            o_ref[0, t * tq:(t + 1) * tq, :] = on.T.astype(BF16)


def _kv_tile(lk):
    best = 128
    for t in range(128, KV_TILE_MAX + 1, 128):
        if lk % t == 0:
            best = t
    return best


def _attn(q, k_all, vt_all, lam, sg_t, lam_init, tq, nt):
    B, L, _ = q.shape
    Lk = k_all.shape[1]
    tk = _kv_tile(Lk)
    return pl.pallas_call(
        functools.partial(_attn_kernel, tq=tq, nt=nt, tk=tk, nk=Lk // tk, post_scale=1.0 - lam_init),
        grid=(B, HEADS, L // (nt * tq)),
        in_specs=[pl.BlockSpec((1, nt * tq, HEAD_W), lambda b, h, i: (b, i, h)),
                  pl.BlockSpec((1, Lk, HEAD_W), lambda b, h, i: (b, 0, h)),
                  pl.BlockSpec((1, HEAD_W, Lk), lambda b, h, i: (b, h, 0)),
                  pl.BlockSpec(lam.shape, lambda b, h, i: (0, 0)),
                  pl.BlockSpec((HEAD_W, tq), lambda b, h, i: (0, 0))],
        out_specs=pl.BlockSpec((1, nt * tq, HEAD_W), lambda b, h, i: (b, i, h)),
        out_shape=jax.ShapeDtypeStruct((B, L, D), BF16),
        scratch_shapes=[pltpu.VMEM((2, tk, 2 * tq), F32), pltpu.VMEM((HEAD_W + ONES_ROWS, 2 * tq), F32)],
        compiler_params=pltpu.CompilerParams(dimension_semantics=("arbitrary",) * 3, vmem_limit_bytes=VMEM_LIMIT),
        name="attn",
    )(q, k_all, vt_all, lam, sg_t)


def _mixout_kernel(o_ref, ga_ref, t_ref, x_ref, mod_ref, gpost_ref, gpre_ref, wba_ref, wout_ref, xn_ref, h2t_ref):
    b = pl.program_id(0)
    y_attn = jnp.dot(o_ref[0], wba_ref[...], preferred_element_type=F32)
    merged = (ga_ref[0].astype(F32) * y_attn + t_ref[0].astype(F32)).astype(BF16)
    y = jnp.dot(merged, wout_ref[...], preferred_element_type=F32)
    gt1 = mod_ref[pl.ds(b, 1), 2 * D:3 * D]
    x_new = x_ref[0] + gt1 * (_rms(y) * gpost_ref[...])
    xn_ref[0] = x_new
    h2 = _prenorm(x_new, gpre_ref, mod_ref, b, 3 * D)
    h2t_ref[0] = h2.T.astype(BF16)


def _mixout(o, ga, t, x, mod, g_post, g_pre_ffn, wba_b, wout_b, tm):
    B, L, _ = x.shape
    full = lambda shape: pl.BlockSpec(shape, lambda b, i: (0,) * len(shape))
    tile = pl.BlockSpec((1, tm, D), lambda b, i: (b, i, 0))
    return pl.pallas_call(
        _mixout_kernel,
        grid=(B, L // tm),
        in_specs=[tile, tile, tile, tile, full(mod.shape), full((1, D)), full((1, D)), full((D, D)), full((D, D))],
        out_specs=[tile, pl.BlockSpec((1, D, tm), lambda b, i: (b, 0, i))],
        out_shape=[jax.ShapeDtypeStruct((B, L, D), F32), jax.ShapeDtypeStruct((B, D, L), BF16)],
        compiler_params=pltpu.CompilerParams(dimension_semantics=("arbitrary", "arbitrary"),
                                             vmem_limit_bytes=VMEM_LIMIT),
        name="mixout",
    )(o, ga, t, x, mod, g_post, g_pre_ffn, wba_b, wout_b)


PEER_SUB = 4 * PEER_KEYS
PEER_TILE = (64, 128)
PEER_TOKEN_SPLIT = 1


def _sorting_network(n):
    pairs, p = [], 1
    while p < n:
        k = p
        while k >= 1:
            for j in range(k % p, n - k, 2 * k):
                for i in range(min(k, n - j - k)):
                    if (i + j) // (2 * p) == (i + j + k) // (2 * p):
                        pairs.append((i + j, i + j + k))
            k //= 2
        p *= 2
    return pairs


def _col_reduce8(x, op):
    for shift in (4, 2, 1):
        x = op(x, pltpu.roll(x, shift, 0))
    return x


def _slabs(x):
    return [x[8 * k:8 * (k + 1), :] for k in range(x.shape[0] // 8)]


def _top16_desc(s, top_scr, p):
    lists = _slabs(s)
    n = len(lists)
    for i, j in _sorting_network(n):
        lists[i], lists[j] = jnp.maximum(lists[i], lists[j]), jnp.minimum(lists[i], lists[j])
    for r in range(PEER_TOPK):
        m = _col_reduce8(lists[0], jnp.maximum)
        top_scr[p, r:r + 1, :] = m[0:1, :]
        if r + 1 < PEER_TOPK:
            hit = lists[0] == m
            for i in range(n - 1 - r):
                lists[i] = jnp.where(hit, lists[i + 1], lists[i])


def _peer_score(h2t, wq_ref, keys_ref, th_scr, t2_scr, r1_scr, top_scr, tt):
    qt = jnp.dot(wq_ref[...], h2t, preferred_element_type=F32)
    neg_inf = jnp.float32(-jnp.inf)
    rep = lambda p, k: jnp.broadcast_to(top_scr[p, k:k + 1, :], (8, tt))
    for h in range(PEER_HEADS):
        halves = []
        for p in range(2):
            r0 = (2 * h + p) * PEER_KEYS
            qhp = qt[r0:r0 + PEER_KEYS, :]
            qn = qhp * lax.rsqrt(jnp.mean(qhp * qhp, axis=0, keepdims=True) + RMS_EPS)
            s = jnp.dot(keys_ref[p], qn.astype(BF16), preferred_element_type=F32)
            halves.append(_slabs(s))
            _top16_desc(s, top_scr, p)
        s1, s2 = halves
        a1_lo, a1_hi = top_scr[0, 0:8, :], top_scr[0, 8:16, :]
        a2_lo, a2_hi = top_scr[1, 0:8, :], top_scr[1, 8:16, :]
        cands = [rep(0, 0) + a2_lo, rep(0, 0) + a2_hi, a1_hi + rep(1, 0)]
        cands += [rep(0, j) + a2_lo for j in range(1, 8)]
        work = list(cands)
        tau = None
        for r in range(PEER_TOPK):
            m = _col_reduce8(functools.reduce(jnp.maximum, work), jnp.maximum)
            if r + 1 < PEER_TOPK:
                work = [jnp.where(c == m, neg_inf, c) for c in work]
            else:
                tau = m
        cmax = rep(0, 0) + rep(1, 0)
        zsum = _col_reduce8(
            functools.reduce(lambda a, c: a + c, [jnp.where(c >= tau, jnp.exp(c - cmax), 0.0) for c in cands]),
            lambda a, c: a + c)
        theta = [jnp.full((8, tt), jnp.inf, F32)] * len(s1)
        for k in range(PEER_TOPK):
            a2k = rep(1, k)
            passing = jnp.minimum(jnp.where(a1_lo + a2k >= tau, a1_lo, jnp.inf),
                                  jnp.where(a1_hi + a2k >= tau, a1_hi, jnp.inf))
            bound = _col_reduce8(passing, jnp.minimum)
            theta = [jnp.where(s >= bound, a2k, t) for s, t in zip(s1, theta)]
        lz = jnp.log2(zsum)
        a1_max, a2_max = rep(0, 0), rep(1, 0)
        lz = lz + 1.0
        for i in range(len(s1)):
            rows = slice(8 * i, 8 * (i + 1))
            th_scr[h, rows, :] = (theta[i] - a2_max) * LOG2E - lz
            t2_scr[h, rows, :] = (s2[i] - a2_max) * LOG2E - lz
            r1_scr[h, rows, :] = (s1[i] - a1_max) * LOG2E


def _peer_kernel(h2t_ref, xn_ref, mod_ref, gpost_ref, wq_ref, keys_ref, u_ref, vt_ref, out_ref,
                 th_scr, t2_scr, r1_scr, top_scr, acc_scr, a_scr, w_scr, thc_scr, r1c_scr, *, tt, ch):
    b = pl.program_id(0)
    c = pl.program_id(2)

    @pl.when(c == 0)
    def _():
        _peer_score(h2t_ref[0], wq_ref, keys_ref, th_scr, t2_scr, r1_scr, top_scr, tt)
        acc_scr[...] = jnp.zeros(acc_scr.shape, F32)

    tr, tl = PEER_TILE
    nj = PEER_SUB // PEER_KEYS
    nsub = ch // PEER_SUB
    step_rows = pl.ds(pl.multiple_of(c * (ch // PEER_KEYS), 8), ch // PEER_KEYS)
    thc_scr[...] = th_scr[:, step_rows, :]
    r1c_scr[...] = r1_scr[:, step_rows, :]

    tw = tt // PEER_TOKEN_SPLIT
    stages = [(sc, hf) for sc in range(nsub) for hf in range(PEER_TOKEN_SPLIT)]

    def experts(sc):
        return slice(sc * PEER_SUB, (sc + 1) * PEER_SUB)

    def tokens(hf):
        return slice(hf * tw, (hf + 1) * tw)

    def activations(s):
        sc, hf = stages[s]
        x = jnp.dot(u_ref[experts(sc), :], h2t_ref[0, :, tokens(hf)], preferred_element_type=F32)
        a_scr[s % 2] = x * (1.0 + lax.erf(x * INV_SQRT2))

    def mix_out(s):
        sc, hf = stages[s]
        acc_scr[:, tokens(hf)] += jnp.dot(vt_ref[:, experts(sc)], w_scr[s % 2], preferred_element_type=F32)

    activations(0)
    for s, (sc, hf) in enumerate(stages):
        if s + 1 < len(stages):
            activations(s + 1)
        if s > 0:
            mix_out(s - 1)
        for rt in range(PEER_KEYS // tr):
            for lt in range(tw // tl):
                r = slice(rt * tr, (rt + 1) * tr)
                l = slice(hf * tw + lt * tl, hf * tw + (lt + 1) * tl)
                ls = slice(lt * tl, (lt + 1) * tl)
                g = [None] * nj
                for h in range(PEER_HEADS):
                    t2t = t2_scr[h, r, l]
                    for j in range(nj):
                        row = slice(sc * nj + j, sc * nj + j + 1)
                        gate = jnp.where(t2t >= thc_scr[h, row, l], jnp.exp2(t2t + r1c_scr[h, row, l]), 0.0)
                        g[j] = gate if h == 0 else g[j] + gate
                for j in range(nj):
                    rows = slice(j * PEER_KEYS + rt * tr, j * PEER_KEYS + (rt + 1) * tr)
                    w_scr[s % 2, rows, ls] = (a_scr[s % 2, rows, ls] * g[j]).astype(BF16)
    mix_out(len(stages) - 1)

    @pl.when(c == pl.num_programs(2) - 1)
    def _():
        y = acc_scr[...].T
        gt2 = mod_ref[pl.ds(b, 1), 5 * D:6 * D]
        out_ref[0] = xn_ref[0] + gt2 * (_rms(y) * gpost_ref[...])


def _peer(h2t, x_new, mod, g_post, wq_t, keys_b, u_b, vt_b, tt, ch):
    B, _, L = h2t.shape
    full = lambda shape: pl.BlockSpec(shape, lambda b, i, c: (0,) * len(shape))
    tile = pl.BlockSpec((1, tt, D), lambda b, i, c: (b, i, 0))
    per_head = pltpu.VMEM((PEER_HEADS, PEER_KEYS, tt), F32)
    nsub = ch // PEER_SUB
    return pl.pallas_call(
        functools.partial(_peer_kernel, tt=tt, ch=ch),
        grid=(B, L // tt, PEER_EXPERTS // ch),
        in_specs=[pl.BlockSpec((1, D, tt), lambda b, i, c: (b, 0, i), pipeline_mode=pl.Buffered(1)),
                  pl.BlockSpec((1, tt, D), lambda b, i, c: (b, i, 0), pipeline_mode=pl.Buffered(1)),
                  full(mod.shape), full((1, D)),
                  pl.BlockSpec(wq_t.shape, lambda b, i, c: (0, 0), pipeline_mode=pl.Buffered(1)), full(keys_b.shape),
                  pl.BlockSpec((ch, D), lambda b, i, c: (c, 0)),
                  pl.BlockSpec((D, ch), lambda b, i, c: (0, c))],
        out_specs=tile,
        out_shape=jax.ShapeDtypeStruct((B, L, D), F32),
        scratch_shapes=[per_head, per_head, per_head, pltpu.VMEM((2, PEER_TOPK, tt), F32),
                        pltpu.VMEM((D, tt), F32), pltpu.VMEM((2, PEER_SUB, tt // PEER_TOKEN_SPLIT), F32),
                        pltpu.VMEM((2, PEER_SUB, tt // PEER_TOKEN_SPLIT), BF16),
                        pltpu.VMEM((PEER_HEADS, ch // PEER_KEYS, tt), F32),
                        pltpu.VMEM((PEER_HEADS, ch // PEER_KEYS, tt), F32)],
        compiler_params=pltpu.CompilerParams(dimension_semantics=("arbitrary",) * 3,
                                             vmem_limit_bytes=PEER_VMEM_LIMIT),
        name="peer",
    )(h2t, x_new, mod, g_post, wq_t, keys_b, u_b, vt_b)


def _rope_tables(L):
    nfreq = DH // 4
    inv = (ROPE_BASE ** (-np.arange(nfreq, dtype=np.float32) / nfreq)).astype(np.float32)
    t = np.arange(L, dtype=np.int32)
    rows = (t // GRID_W).astype(np.float32)[:, None] * inv[None, :]
    cols = (t % GRID_W).astype(np.float32)[:, None] * inv[None, :]
    cos64 = np.concatenate([np.cos(rows), np.cos(rows), np.cos(cols), np.cos(cols)], axis=1)
    sin64 = np.concatenate([-np.sin(rows), np.sin(rows), -np.sin(cols), np.sin(cols)], axis=1)
    return jnp.asarray(np.tile(cos64, (1, 2)), F32), jnp.asarray(np.tile(sin64, (1, 2)), F32)


def kernel(x, c, ctx, c_ctx, w_ada, b_ada, g_pre_mix, g_post_mix, g_pre_ffn, g_post_ffn, w_in, b_gate, lambda_q1, lambda_k1, lambda_q2, lambda_k2, subln_g, sgu_ln_g, sgu_ln_b, sgu_w, sgu_b, w_branch_attn, w_branch_sgu, w_out, peer_w_query, peer_sub_keys, peer_u, peer_v):
    B, L, _ = x.shape
    depth = w_ada.shape[0]
    assert depth == 1 and B <= 7 and L % 512 == 0
    lam_init = 0.8 - 0.6 * math.exp(-0.3 * 0)
    tm = 512 if L % 512 == 0 else 256

    row = lambda a: a[0].reshape(1, -1)
    cc = jnp.concatenate([c, c_ctx[None, :], jnp.zeros((8 - B - 1, D), F32)], axis=0)
    mod, lam = _ada(cc, w_ada[0], row(b_ada), row(lambda_q1), row(lambda_k1), row(lambda_q2), row(lambda_k2), lam_init)

    win_b = w_in[0].astype(BF16)
    cos_t, sin_t = _rope_tables(L)
    sb_b = jnp.broadcast_to(sgu_b[0][:, :, None], (SGU_GROUPS, SGU_CHUNK, SGU_W // SGU_GROUPS))
    q, k, vt, t_sgu, g_attn = _inproj(x, mod, row(g_pre_mix), win_b, row(b_gate), cos_t, sin_t,
                                      row(sgu_ln_g), row(sgu_ln_b), sgu_w[0].astype(BF16), sb_b,
                                      w_branch_sgu[0].astype(BF16), tm)
    kc, vct = _ctxkv(ctx, mod, row(g_pre_mix), win_b, B)
    k_all = jnp.concatenate([kc, k], axis=1)
    vt_all = jnp.concatenate([vct, vt], axis=2)
    tq = 256
    sg_t = jnp.broadcast_to(subln_g[0][:, None], (HEAD_W, tq))
    o = _attn(q, k_all, vt_all, lam, sg_t, lam_init, tq, 8 if L % (8 * tq) == 0 else 2)

    x_new, h2t = _mixout(o, g_attn, t_sgu, x, mod, row(g_post_mix), row(g_pre_ffn),
                         w_branch_attn[0].astype(BF16), w_out[0].astype(BF16), tm)
    return _peer(h2t, x_new, mod, row(g_post_ffn), peer_w_query[0].T.astype(BF16),
                 peer_sub_keys[0].astype(BF16), peer_u[0].astype(BF16), peer_v[0].T.astype(BF16), 512, 4096)
```

```python
import functools
import math

import jax
import jax.numpy as jnp
from jax import lax
from jax.experimental import pallas as pl
from jax.experimental.pallas import tpu as pltpu

F32 = jnp.float32
BF16 = jnp.bfloat16

D = 1024
N_ADA = 6
GRID_W = 64
HEADS = 8
DH = 64
HEAD_W = 2 * DH
ROPE_BASE = 10000.0
SGU_CHUNK = 128
SGU_GROUPS = 8
SGU_W = 1024
Q_W = K_W = V_W = 1024
Z_W = 2 * SGU_W
Z_OFF = Q_W + K_W + V_W
G_OFF = Z_OFF + Z_W
PEER_HEADS = 8
PEER_KEYS = 128
PEER_TOPK = 16
PEER_EXPERTS = PEER_KEYS * PEER_KEYS
RMS_EPS = 1e-6
LN_EPS = 1e-5
LOG2E = 1.4426950408889634
INV_SQRT2 = 0.7071067811865476
NEG_BIG = -1e30
LAM_W = 512

VMEM_LIMIT = 56 * 1024 * 1024


def _gelu(x):
    return 0.5 * x * (1.0 + lax.erf(x * INV_SQRT2))


def _rms(x, eps=RMS_EPS):
    return x * lax.rsqrt(jnp.mean(x * x, axis=-1, keepdims=True) + eps)


def _ada_kernel(cc_ref, w_ref, b_ref, lq1_ref, lk1_ref, lq2_ref, lk2_ref, mod_ref, lam_ref, *, lam_init):
    a = cc_ref[...]
    s = (a * jax.nn.sigmoid(a)).astype(BF16)
    mod_ref[...] = jnp.dot(s, w_ref[...].astype(BF16), preferred_element_type=F32) + b_ref[...]
    d1 = jnp.sum(lq1_ref[...] * lk1_ref[...], axis=-1, keepdims=True)
    d2 = jnp.sum(lq2_ref[...] * lk2_ref[...], axis=-1, keepdims=True)
    lam = jnp.exp(d1) - jnp.exp(d2) + lam_init
    lam_ref[...] = jnp.broadcast_to(lam, lam_ref.shape)


def _ada(cc, w_ada, b_ada, lq1, lk1, lq2, lk2, lam_init):
    tn = 1536
    n = w_ada.shape[1]
    vec = pl.BlockSpec((1, DH), lambda j: (0, 0))
    return pl.pallas_call(
        functools.partial(_ada_kernel, lam_init=lam_init),
        grid=(n // tn,),
        in_specs=[pl.BlockSpec((8, D), lambda j: (0, 0)),
                  pl.BlockSpec((D, tn), lambda j: (0, j)),
                  pl.BlockSpec((1, tn), lambda j: (0, j)),
                  vec, vec, vec, vec],
        out_specs=[pl.BlockSpec((8, tn), lambda j: (0, j)),
                   pl.BlockSpec((8, LAM_W), lambda j: (0, 0))],
        out_shape=[jax.ShapeDtypeStruct((8, n), F32), jax.ShapeDtypeStruct((8, LAM_W), F32)],
        compiler_params=pltpu.CompilerParams(dimension_semantics=("arbitrary",), vmem_limit_bytes=VMEM_LIMIT),
        name="ada",
    )(cc, w_ada, b_ada, lq1, lk1, lq2, lk2)


def _prenorm(x, g_ref, mod_ref, row, col0):
    sh = mod_ref[pl.ds(row, 1), col0:col0 + D]
    sc = mod_ref[pl.ds(row, 1), col0 + D:col0 + 2 * D]
    return _rms(x) * g_ref[...] * (1.0 + sc) + sh


def _inproj_kernel(x_ref, mod_ref, g_ref, win_ref, bg_ref, cos_ref, sin_ref, lng_ref, lnb_ref,
                   sw_ref, sb_ref, wbs_ref, q_ref, k_ref, vt_ref, t_ref, ga_ref, s_scr, *, tm):
    b = pl.program_id(0)
    h = _prenorm(x_ref[0], g_ref, mod_ref, b, 0).astype(BF16)

    lane = lax.broadcasted_iota(jnp.int32, (tm, 128), 1)
    first = (lane & 16) == 0
    cosv = cos_ref[...]
    sinv = sin_ref[...]

    def rope(t):
        swapped = jnp.where(first, pltpu.roll(t, 112, 1), pltpu.roll(t, 16, 1))
        return t * cosv + swapped * sinv

    q = jnp.dot(h, win_ref[:, 0:Q_W], preferred_element_type=F32)
    for j in range(Q_W // 128):
        q_ref[0, :, j * 128:(j + 1) * 128] = (rope(q[:, j * 128:(j + 1) * 128]) * (DH ** -0.5 * LOG2E)).astype(BF16)
    k = jnp.dot(h, win_ref[:, Q_W:Q_W + K_W], preferred_element_type=F32)
    for j in range(K_W // 128):
        k_ref[0, :, j * 128:(j + 1) * 128] = rope(k[:, j * 128:(j + 1) * 128]).astype(BF16)
    vt_ref[0] = jnp.dot(h, win_ref[:, Q_W + K_W:Z_OFF], preferred_element_type=F32).T.astype(BF16)

    z = _gelu(jnp.dot(h, win_ref[:, Z_OFF:G_OFF], preferred_element_type=F32))
    u = z[:, :SGU_W]
    vv = z[:, SGU_W:]
    mu = jnp.mean(vv, axis=-1, keepdims=True)
    xc = vv - mu
    vln = (xc * lax.rsqrt(jnp.mean(xc * xc, axis=-1, keepdims=True) + LN_EPS) * lng_ref[...] + lnb_ref[...]).astype(BF16)
    nc = tm // SGU_CHUNK
    gw = SGU_W // SGU_GROUPS
    for g in range(SGU_GROUPS):
        rhs = jnp.concatenate([vln[n * SGU_CHUNK:(n + 1) * SGU_CHUNK, g * gw:(g + 1) * gw] for n in range(nc)], axis=1)
        mixed = jnp.dot(sw_ref[g], rhs, preferred_element_type=F32)
        for n in range(nc):
            rows = slice(n * SGU_CHUNK, (n + 1) * SGU_CHUNK)
            cols = slice(g * gw, (g + 1) * gw)
            s_scr[rows, cols] = (u[rows, cols] * (mixed[:, n * gw:(n + 1) * gw] + sb_ref[g])).astype(BF16)
    y_sgu = jnp.dot(s_scr[...], wbs_ref[...], preferred_element_type=F32)

    gate = jax.nn.sigmoid(jnp.dot(h, win_ref[:, G_OFF:G_OFF + 2 * D], preferred_element_type=F32) + bg_ref[...])
    ga_ref[0] = gate[:, :D].astype(BF16)
    t_ref[0] = (gate[:, D:] * y_sgu).astype(BF16)


def _inproj(x, mod, g_pre, win_b, b_gate, cos_t, sin_t, ln_g, ln_b, sw_b, sb_b, wbs_b, tm):
    B, L, _ = x.shape
    full = lambda shape: pl.BlockSpec(shape, lambda b, i: (0,) * len(shape))
    tile = pl.BlockSpec((1, tm, D), lambda b, i: (b, i, 0))
    tile_t = pl.BlockSpec((1, D, tm), lambda b, i: (b, 0, i))
    out = jax.ShapeDtypeStruct((B, L, D), BF16)
    out_t = jax.ShapeDtypeStruct((B, D, L), BF16)
    return pl.pallas_call(
        functools.partial(_inproj_kernel, tm=tm),
        grid=(B, L // tm),
        in_specs=[tile, full(mod.shape), full((1, D)), full(win_b.shape), full((1, 2 * D)),
                  pl.BlockSpec((tm, 128), lambda b, i: (i, 0)), pl.BlockSpec((tm, 128), lambda b, i: (i, 0)),
                  full((1, SGU_W)), full((1, SGU_W)), full(sw_b.shape), full(sb_b.shape), full(wbs_b.shape)],
        out_specs=[tile, tile, tile_t, tile, tile],
        out_shape=[out, out, out_t, out, out],
        scratch_shapes=[pltpu.VMEM((tm, SGU_W), BF16)],
        compiler_params=pltpu.CompilerParams(dimension_semantics=("arbitrary", "arbitrary"),
                                             vmem_limit_bytes=VMEM_LIMIT),
        name="inproj",
    )(x, mod, g_pre, win_b, b_gate, cos_t, sin_t, ln_g, ln_b, sw_b, sb_b, wbs_b)


def _ctxkv_kernel(ctx_ref, mod_ref, g_ref, wk_ref, wv_ref, k_ref, vt_ref, *, ctx_row):
    h = _prenorm(ctx_ref[0], g_ref, mod_ref, ctx_row, 0).astype(BF16)
    k_ref[0] = jnp.dot(h, wk_ref[...], preferred_element_type=F32).astype(BF16)
    vt_ref[0] = jnp.dot(h, wv_ref[...], preferred_element_type=F32).T.astype(BF16)


def _ctxkv(ctx, mod, g_pre, win_b, ctx_row):
    B, Lc, _ = ctx.shape
    tile = pl.BlockSpec((1, Lc, D), lambda b: (b, 0, 0))
    return pl.pallas_call(
        functools.partial(_ctxkv_kernel, ctx_row=ctx_row),
        grid=(B,),
        in_specs=[tile, pl.BlockSpec(mod.shape, lambda b: (0, 0)), pl.BlockSpec((1, D), lambda b: (0, 0)),
                  pl.BlockSpec((D, K_W), lambda b: (0, Q_W // K_W)),
                  pl.BlockSpec((D, V_W), lambda b: (0, (Q_W + K_W) // V_W))],
        out_specs=[tile, pl.BlockSpec((1, D, Lc), lambda b: (b, 0, 0))],
        out_shape=[jax.ShapeDtypeStruct((B, Lc, D), BF16), jax.ShapeDtypeStruct((B, D, Lc), BF16)],
        compiler_params=pltpu.CompilerParams(dimension_semantics=("arbitrary",), vmem_limit_bytes=VMEM_LIMIT),
        name="ctxkv",
    )(ctx, mod, g_pre, win_b, win_b)


ONES_ROWS = 16
KV_TILE_MAX = 4224


def _attn_kernel(q_ref, k_ref, vt_ref, lam_ref, sgt_ref, o_ref, s_scr, acc_scr, *, tq, nt, tk, nk, post_scale):
    lane = lax.broadcasted_iota(jnp.int32, (tq, HEAD_W), 1)
    ones = jnp.ones((ONES_ROWS, tk), BF16)

    def stacked_q(t):
        q = q_ref[0, t * tq:(t + 1) * tq, :]
        zero = jnp.zeros_like(q)
        return jnp.concatenate([jnp.where(lane < DH, q, zero), jnp.where(lane >= DH, q, zero)], axis=0)

    def scores(seq):
        t, j = divmod(seq, nk)
        kb = k_ref[0, j * tk:(j + 1) * tk, :]
        st = lax.dot_general(kb, stacked_q(t), (((1,), (1,)), ((), ())), preferred_element_type=F32)
        s_scr[seq % 2] = st
        return jnp.max(st, axis=0, keepdims=True)

    blk_max = scores(0)
    for seq in range(nt * nk):
        t, j = divmod(seq, nk)
        nxt_max = scores(seq + 1) if seq + 1 < nt * nk else None
        if j == 0:
            m = jnp.full((1, 2 * tq), NEG_BIG, F32)
        m_new = jnp.maximum(m, blk_max)
        pt = jnp.exp2(s_scr[seq % 2] - m_new).astype(BF16)
        vte = jnp.concatenate([vt_ref[0, :, j * tk:(j + 1) * tk], ones], axis=0)
        pv = jnp.dot(vte, pt, preferred_element_type=F32)
        acc_scr[...] = pv if j == 0 else jnp.exp2(m - m_new) * acc_scr[...] + pv
        m, blk_max = m_new, nxt_max
        if j == nk - 1:
            acc = acc_scr[...]
            o1 = acc[:HEAD_W, :tq] / acc[HEAD_W:HEAD_W + 1, :tq]
            o2 = acc[:HEAD_W, tq:] / acc[HEAD_W:HEAD_W + 1, tq:]
            ot = o1 - lam_ref[0:1, 0:tq] * o2
            on = ot * lax.rsqrt(jnp.mean(ot * ot, axis=0, keepdims=True) + RMS_EPS) * sgt_ref[...] * post_scale
            o_ref[0, t * tq:(t + 1) * tq, :] = on.T.astype(BF16)


def _kv_tile(lk):
    best = 128
    for t in range(128, KV_TILE_MAX + 1, 128):
        if lk % t == 0:
            best = t
    return best


def _attn(q, k_all, vt_all, lam, sg_t, lam_init, tq, nt):
    B, L, _ = q.shape
    Lk = k_all.shape[1]
    tk = _kv_tile(Lk)
    return pl.pallas_call(
        functools.partial(_attn_kernel, tq=tq, nt=nt, tk=tk, nk=Lk // tk, post_scale=1.0 - lam_init),
        grid=(B, HEADS, L // (nt * tq)),
        in_specs=[pl.BlockSpec((1, nt * tq, HEAD_W), lambda b, h, i: (b, i, h)),
                  pl.BlockSpec((1, Lk, HEAD_W), lambda b, h, i: (b, 0, h)),
                  pl.BlockSpec((1, HEAD_W, Lk), lambda b, h, i: (b, h, 0)),
                  pl.BlockSpec(lam.shape, lambda b, h, i: (0, 0)),
                  pl.BlockSpec((HEAD_W, tq), lambda b, h, i: (0, 0))],
        out_specs=pl.BlockSpec((1, nt * tq, HEAD_W), lambda b, h, i: (b, i, h)),
        out_shape=jax.ShapeDtypeStruct((B, L, D), BF16),
        scratch_shapes=[pltpu.VMEM((2, tk, 2 * tq), F32), pltpu.VMEM((HEAD_W + ONES_ROWS, 2 * tq), F32)],
        compiler_params=pltpu.CompilerParams(dimension_semantics=("arbitrary",) * 3, vmem_limit_bytes=VMEM_LIMIT),
        name="attn",
    )(q, k_all, vt_all, lam, sg_t)


def _mixout_kernel(o_ref, ga_ref, t_ref, x_ref, mod_ref, gpost_ref, gpre_ref, wba_ref, wout_ref, xn_ref, h2t_ref):
    b = pl.program_id(0)
    y_attn = jnp.dot(o_ref[0], wba_ref[...], preferred_element_type=F32)
    merged = (ga_ref[0].astype(F32) * y_attn + t_ref[0].astype(F32)).astype(BF16)
    y = jnp.dot(merged, wout_ref[...], preferred_element_type=F32)
    gt1 = mod_ref[pl.ds(b, 1), 2 * D:3 * D]
    x_new = x_ref[0] + gt1 * (_rms(y) * gpost_ref[...])
    xn_ref[0] = x_new
    h2 = _prenorm(x_new, gpre_ref, mod_ref, b, 3 * D)
    h2t_ref[0] = h2.T.astype(BF16)


def _mixout(o, ga, t, x, mod, g_post, g_pre_ffn, wba_b, wout_b, tm):
    B, L, _ = x.shape
    full = lambda shape: pl.BlockSpec(shape, lambda b, i: (0,) * len(shape))
    tile = pl.BlockSpec((1, tm, D), lambda b, i: (b, i, 0))
    return pl.pallas_call(
        _mixout_kernel,
        grid=(B, L // tm),
        in_specs=[tile, tile, tile, tile, full(mod.shape), full((1, D)), full((1, D)), full((D, D)), full((D, D))],
        out_specs=[tile, pl.BlockSpec((1, D, tm), lambda b, i: (b, 0, i))],
        out_shape=[jax.ShapeDtypeStruct((B, L, D), F32), jax.ShapeDtypeStruct((B, D, L), BF16)],
        compiler_params=pltpu.CompilerParams(dimension_semantics=("arbitrary", "arbitrary"),
                                             vmem_limit_bytes=VMEM_LIMIT),
        name="mixout",
    )(o, ga, t, x, mod, g_post, g_pre_ffn, wba_b, wout_b)


PEER_SUB = 4 * PEER_KEYS
PEER_TILE = (64, 128)
PEER_TOKEN_SPLIT = 1


def _sorting_network(n):
    pairs, p = [], 1
    while p < n:
        k = p
        while k >= 1:
            for j in range(k % p, n - k, 2 * k):
                for i in range(min(k, n - j - k)):
                    if (i + j) // (2 * p) == (i + j + k) // (2 * p):
                        pairs.append((i + j, i + j + k))
            k //= 2
        p *= 2
    return pairs


def _col_reduce8(x, op):
    for shift in (4, 2, 1):
        x = op(x, pltpu.roll(x, shift, 0))
    return x


def _slabs(x):
    return [x[8 * k:8 * (k + 1), :] for k in range(x.shape[0] // 8)]


def _top16_desc(s, top_scr, p):
    lists = _slabs(s)
    n = len(lists)
    for i, j in _sorting_network(n):
        lists[i], lists[j] = jnp.maximum(lists[i], lists[j]), jnp.minimum(lists[i], lists[j])
    for r in range(PEER_TOPK):
        m = _col_reduce8(lists[0], jnp.maximum)
        top_scr[p, r:r + 1, :] = m[0:1, :]
        if r + 1 < PEER_TOPK:
            hit = lists[0] == m
            for i in range(n - 1 - r):
                lists[i] = jnp.where(hit, lists[i + 1], lists[i])


def _peer_score(h2t, wq_ref, keys_ref, th_scr, t2_scr, r1_scr, top_scr, tt):
    qt = jnp.dot(wq_ref[...], h2t, preferred_element_type=F32)
    neg_inf = jnp.float32(-jnp.inf)
    rep = lambda p, k: jnp.broadcast_to(top_scr[p, k:k + 1, :], (8, tt))
    for h in range(PEER_HEADS):
        halves = []
        for p in range(2):
            r0 = (2 * h + p) * PEER_KEYS
            qhp = qt[r0:r0 + PEER_KEYS, :]
            qn = qhp * lax.rsqrt(jnp.mean(qhp * qhp, axis=0, keepdims=True) + RMS_EPS)
            s = jnp.dot(keys_ref[p], qn.astype(BF16), preferred_element_type=F32)
            halves.append(_slabs(s))
            _top16_desc(s, top_scr, p)
        s1, s2 = halves
        a1_lo, a1_hi = top_scr[0, 0:8, :], top_scr[0, 8:16, :]
        a2_lo, a2_hi = top_scr[1, 0:8, :], top_scr[1, 8:16, :]
        cands = [rep(0, 0) + a2_lo, rep(0, 0) + a2_hi, a1_hi + rep(1, 0)]
        cands += [rep(0, j) + a2_lo for j in range(1, 8)]
        work = list(cands)
        tau = None
        for r in range(PEER_TOPK):
            m = _col_reduce8(functools.reduce(jnp.maximum, work), jnp.maximum)
            if r + 1 < PEER_TOPK:
                work = [jnp.where(c == m, neg_inf, c) for c in work]
            else:
                tau = m
        cmax = rep(0, 0) + rep(1, 0)
        zsum = _col_reduce8(
            functools.reduce(lambda a, c: a + c, [jnp.where(c >= tau, jnp.exp(c - cmax), 0.0) for c in cands]),
            lambda a, c: a + c)
        theta = [jnp.full((8, tt), jnp.inf, F32)] * len(s1)
        for k in range(PEER_TOPK):
            a2k = rep(1, k)
            passing = jnp.minimum(jnp.where(a1_lo + a2k >= tau, a1_lo, jnp.inf),
                                  jnp.where(a1_hi + a2k >= tau, a1_hi, jnp.inf))
            bound = _col_reduce8(passing, jnp.minimum)
            theta = [jnp.where(s >= bound, a2k, t) for s, t in zip(s1, theta)]
        lz = jnp.log2(zsum)
        a1_max, a2_max = rep(0, 0), rep(1, 0)
        lz = lz + 1.0
        for i in range(len(s1)):
            rows = slice(8 * i, 8 * (i + 1))
            th_scr[h, rows, :] = (theta[i] - a2_max) * LOG2E - lz
            t2_scr[h, rows, :] = (s2[i] - a2_max) * LOG2E - lz
            r1_scr[h, rows, :] = (s1[i] - a1_max) * LOG2E


def _peer_kernel(h2t_ref, xn_ref, mod_ref, gpost_ref, wq_ref, keys_ref, u_ref, vt_ref, out_ref,
                 th_scr, t2_scr, r1_scr, top_scr, acc_scr, a_scr, w_scr, thc_scr, r1c_scr, *, tt, ch):
    b = pl.program_id(0)
    c = pl.program_id(2)

    @pl.when(c == 0)
    def _():
        _peer_score(h2t_ref[0], wq_ref, keys_ref, th_scr, t2_scr, r1_scr, top_scr, tt)
        acc_scr[...] = jnp.zeros(acc_scr.shape, F32)

    tr, tl = PEER_TILE
    nj = PEER_SUB // PEER_KEYS
    nsub = ch // PEER_SUB
    step_rows = pl.ds(pl.multiple_of(c * (ch // PEER_KEYS), 8), ch // PEER_KEYS)
    thc_scr[...] = th_scr[:, step_rows, :]
    r1c_scr[...] = r1_scr[:, step_rows, :]

    tw = tt // PEER_TOKEN_SPLIT
    stages = [(sc, hf) for sc in range(nsub) for hf in range(PEER_TOKEN_SPLIT)]

    def experts(sc):
        return slice(sc * PEER_SUB, (sc + 1) * PEER_SUB)

    def tokens(hf):
        return slice(hf * tw, (hf + 1) * tw)

    def activations(s):
        sc, hf = stages[s]
        x = jnp.dot(u_ref[experts(sc), :], h2t_ref[0, :, tokens(hf)], preferred_element_type=F32)
        a_scr[s % 2] = x * (1.0 + lax.erf(x * INV_SQRT2))

    def mix_out(s):
        sc, hf = stages[s]
        acc_scr[:, tokens(hf)] += jnp.dot(vt_ref[:, experts(sc)], w_scr[s % 2], preferred_element_type=F32)

    activations(0)
    for s, (sc, hf) in enumerate(stages):
        if s + 1 < len(stages):
            activations(s + 1)
        if s > 0:
            mix_out(s - 1)
        for rt in range(PEER_KEYS // tr):
            for lt in range(tw // tl):
                r = slice(rt * tr, (rt + 1) * tr)
                l = slice(hf * tw + lt * tl, hf * tw + (lt + 1) * tl)
                ls = slice(lt * tl, (lt + 1) * tl)
                g = [None] * nj
                for h in range(PEER_HEADS):
                    t2t = t2_scr[h, r, l]
                    for j in range(nj):
                        row = slice(sc * nj + j, sc * nj + j + 1)
                        gate = jnp.where(t2t >= thc_scr[h, row, l], jnp.exp2(t2t + r1c_scr[h, row, l]), 0.0)
                        g[j] = gate if h == 0 else g[j] + gate
                for j in range(nj):
                    rows = slice(j * PEER_KEYS + rt * tr, j * PEER_KEYS + (rt + 1) * tr)
                    w_scr[s % 2, rows, ls] = (a_scr[s % 2, rows, ls] * g[j]).astype(BF16)
    mix_out(len(stages) - 1)

    @pl.when(c == pl.num_programs(2) - 1)
    def _():
        y = acc_scr[...].T
        gt2 = mod_ref[pl.ds(b, 1), 5 * D:6 * D]
        out_ref[0] = xn_ref[0] + gt2 * (_rms(y) * gpost_ref[...])


def _peer(h2t, x_new, mod, g_post, wq_t, keys_b, u_b, vt_b, tt, ch):
    B, _, L = h2t.shape
    full = lambda shape: pl.BlockSpec(shape, lambda b, i, c: (0,) * len(shape))
    tile = pl.BlockSpec((1, tt, D), lambda b, i, c: (b, i, 0))
    per_head = pltpu.VMEM((PEER_HEADS, PEER_KEYS, tt), F32)
    nsub = ch // PEER_SUB
    return pl.pallas_call(
        functools.partial(_peer_kernel, tt=tt, ch=ch),
        grid=(B, L // tt, PEER_EXPERTS // ch),
        in_specs=[pl.BlockSpec((1, D, tt), lambda b, i, c: (b, 0, i)), tile, full(mod.shape), full((1, D)),
                  full(wq_t.shape), full(keys_b.shape),
                  pl.BlockSpec((ch, D), lambda b, i, c: (c, 0)),
                  pl.BlockSpec((D, ch), lambda b, i, c: (0, c))],
        out_specs=tile,
        out_shape=jax.ShapeDtypeStruct((B, L, D), F32),
        scratch_shapes=[per_head, per_head, per_head, pltpu.VMEM((2, PEER_TOPK, tt), F32),
                        pltpu.VMEM((D, tt), F32), pltpu.VMEM((2, PEER_SUB, tt // PEER_TOKEN_SPLIT), F32),
                        pltpu.VMEM((2, PEER_SUB, tt // PEER_TOKEN_SPLIT), BF16),
                        pltpu.VMEM((PEER_HEADS, ch // PEER_KEYS, tt), F32),
                        pltpu.VMEM((PEER_HEADS, ch // PEER_KEYS, tt), F32)],
        compiler_params=pltpu.CompilerParams(dimension_semantics=("arbitrary",) * 3, vmem_limit_bytes=VMEM_LIMIT),
        name="peer",
    )(h2t, x_new, mod, g_post, wq_t, keys_b, u_b, vt_b)


def _rope_tables(L):
    nfreq = DH // 4
    inv = ROPE_BASE ** (-jnp.arange(nfreq, dtype=F32) / nfreq)
    t = jnp.arange(L, dtype=jnp.int32)
    rows = (t // GRID_W).astype(F32)[:, None] * inv[None, :]
    cols = (t % GRID_W).astype(F32)[:, None] * inv[None, :]
    cos64 = jnp.concatenate([jnp.cos(rows), jnp.cos(rows), jnp.cos(cols), jnp.cos(cols)], axis=1)
    sin64 = jnp.concatenate([-jnp.sin(rows), jnp.sin(rows), -jnp.sin(cols), jnp.sin(cols)], axis=1)
    return jnp.tile(cos64, (1, 2)), jnp.tile(sin64, (1, 2))


def _tile_plan(L):
    token_tile = 512
    query_tile = 256
    per_step = max(n for n in (1, 2, 4, 8, 16) if L % (n * query_tile) == 0)
    plan = dict(token_tile=token_tile, query_tile=query_tile, query_tiles_per_step=per_step,
                peer_token_tile=512, peer_experts_per_step=2048)
    assert L % token_tile == 0 and L % plan["peer_token_tile"] == 0 and L % SGU_CHUNK == 0
    return plan


def kernel(x, c, ctx, c_ctx, w_ada, b_ada, g_pre_mix, g_post_mix, g_pre_ffn, g_post_ffn, w_in, b_gate, lambda_q1, lambda_k1, lambda_q2, lambda_k2, subln_g, sgu_ln_g, sgu_ln_b, sgu_w, sgu_b, w_branch_attn, w_branch_sgu, w_out, peer_w_query, peer_sub_keys, peer_u, peer_v):
    B, L, _ = x.shape
    depth = w_ada.shape[0]
    assert depth == 1 and B <= 7
    lam_init = 0.8 - 0.6 * math.exp(-0.3 * 0)
    tiles = _tile_plan(L)
    tm, tq = tiles["token_tile"], tiles["query_tile"]

    row = lambda a: a[0].reshape(1, -1)
    cc = jnp.concatenate([c, c_ctx[None, :], jnp.zeros((8 - B - 1, D), F32)], axis=0)
    mod, lam = _ada(cc, w_ada[0], row(b_ada), row(lambda_q1), row(lambda_k1), row(lambda_q2), row(lambda_k2), lam_init)

    win_b = w_in[0].astype(BF16)
    cos_t, sin_t = _rope_tables(L)
    sb_b = jnp.broadcast_to(sgu_b[0][:, :, None], (SGU_GROUPS, SGU_CHUNK, SGU_W // SGU_GROUPS))
    q, k, vt, t_sgu, g_attn = _inproj(x, mod, row(g_pre_mix), win_b, row(b_gate), cos_t, sin_t,
                                      row(sgu_ln_g), row(sgu_ln_b), sgu_w[0].astype(BF16), sb_b,
                                      w_branch_sgu[0].astype(BF16), tm)
    kc, vct = _ctxkv(ctx, mod, row(g_pre_mix), win_b, B)
    k_all = jnp.concatenate([kc, k], axis=1)
    vt_all = jnp.concatenate([vct, vt], axis=2)
    sg_t = jnp.broadcast_to(subln_g[0][:, None], (HEAD_W, tq))
    o = _attn(q, k_all, vt_all, lam, sg_t, lam_init, tq, tiles["query_tiles_per_step"])

    x_new, h2t = _mixout(o, g_attn, t_sgu, x, mod, row(g_post_mix), row(g_pre_ffn),
                         w_branch_attn[0].astype(BF16), w_out[0].astype(BF16), tm)
    return _peer(h2t, x_new, mod, row(g_post_ffn), peer_w_query[0].T.astype(BF16),
                 peer_sub_keys[0].astype(BF16), peer_u[0].astype(BF16), peer_v[0].T.astype(BF16),
                 tiles["peer_token_tile"], tiles["peer_experts_per_step"])
```

```python
import functools
import math

import jax
import jax.numpy as jnp
from jax import lax
from jax.experimental import pallas as pl
from jax.experimental.pallas import tpu as pltpu

F32 = jnp.float32
BF16 = jnp.bfloat16

D = 1024
N_ADA = 6
GRID_W = 64
HEADS = 8
DH = 64
HEAD_W = 2 * DH
ROPE_BASE = 10000.0
SGU_CHUNK = 128
SGU_GROUPS = 8
SGU_W = 1024
Q_W = K_W = V_W = 1024
Z_W = 2 * SGU_W
Z_OFF = Q_W + K_W + V_W
G_OFF = Z_OFF + Z_W
PEER_HEADS = 8
PEER_KEYS = 128
PEER_TOPK = 16
PEER_EXPERTS = PEER_KEYS * PEER_KEYS
RMS_EPS = 1e-6
LN_EPS = 1e-5
LOG2E = 1.4426950408889634
INV_SQRT2 = 0.7071067811865476
NEG_BIG = -1e30
LAM_W = 512

VMEM_LIMIT = 56 * 1024 * 1024


def _gelu(x):
    return 0.5 * x * (1.0 + lax.erf(x * INV_SQRT2))


def _rms(x, eps=RMS_EPS):
    return x * lax.rsqrt(jnp.mean(x * x, axis=-1, keepdims=True) + eps)


def _ada_kernel(cc_ref, w_ref, b_ref, lq1_ref, lk1_ref, lq2_ref, lk2_ref, mod_ref, lam_ref, *, lam_init):
    a = cc_ref[...]
    s = (a * jax.nn.sigmoid(a)).astype(BF16)
    mod_ref[...] = jnp.dot(s, w_ref[...].astype(BF16), preferred_element_type=F32) + b_ref[...]
    d1 = jnp.sum(lq1_ref[...] * lk1_ref[...], axis=-1, keepdims=True)
    d2 = jnp.sum(lq2_ref[...] * lk2_ref[...], axis=-1, keepdims=True)
    lam = jnp.exp(d1) - jnp.exp(d2) + lam_init
    lam_ref[...] = jnp.broadcast_to(lam, lam_ref.shape)


def _ada(cc, w_ada, b_ada, lq1, lk1, lq2, lk2, lam_init):
    tn = 1536
    n = w_ada.shape[1]
    vec = pl.BlockSpec((1, DH), lambda j: (0, 0))
    return pl.pallas_call(
        functools.partial(_ada_kernel, lam_init=lam_init),
        grid=(n // tn,),
        in_specs=[pl.BlockSpec((8, D), lambda j: (0, 0)),
                  pl.BlockSpec((D, tn), lambda j: (0, j)),
                  pl.BlockSpec((1, tn), lambda j: (0, j)),
                  vec, vec, vec, vec],
        out_specs=[pl.BlockSpec((8, tn), lambda j: (0, j)),
                   pl.BlockSpec((8, LAM_W), lambda j: (0, 0))],
        out_shape=[jax.ShapeDtypeStruct((8, n), F32), jax.ShapeDtypeStruct((8, LAM_W), F32)],
        compiler_params=pltpu.CompilerParams(dimension_semantics=("arbitrary",), vmem_limit_bytes=VMEM_LIMIT),
        name="ada",
    )(cc, w_ada, b_ada, lq1, lk1, lq2, lk2)


def _prenorm(x, g_ref, mod_ref, row, col0):
    sh = mod_ref[pl.ds(row, 1), col0:col0 + D]
    sc = mod_ref[pl.ds(row, 1), col0 + D:col0 + 2 * D]
    return _rms(x) * g_ref[...] * (1.0 + sc) + sh


def _inproj_kernel(x_ref, mod_ref, g_ref, win_ref, bg_ref, cos_ref, sin_ref, lng_ref, lnb_ref,
                   sw_ref, sb_ref, wbs_ref, q_ref, k_ref, vt_ref, t_ref, ga_ref, s_scr, *, tm):
    b = pl.program_id(0)
    h = _prenorm(x_ref[0], g_ref, mod_ref, b, 0).astype(BF16)

    lane = lax.broadcasted_iota(jnp.int32, (tm, 128), 1)
    first = (lane & 16) == 0
    cosv = cos_ref[...]
    sinv = sin_ref[...]

    def rope(t):
        swapped = jnp.where(first, pltpu.roll(t, 112, 1), pltpu.roll(t, 16, 1))
        return t * cosv + swapped * sinv

    q = jnp.dot(h, win_ref[:, 0:Q_W], preferred_element_type=F32)
    for j in range(Q_W // 128):
        q_ref[0, :, j * 128:(j + 1) * 128] = (rope(q[:, j * 128:(j + 1) * 128]) * (DH ** -0.5 * LOG2E)).astype(BF16)
    k = jnp.dot(h, win_ref[:, Q_W:Q_W + K_W], preferred_element_type=F32)
    for j in range(K_W // 128):
        k_ref[0, :, j * 128:(j + 1) * 128] = rope(k[:, j * 128:(j + 1) * 128]).astype(BF16)
    vt_ref[0] = jnp.dot(h, win_ref[:, Q_W + K_W:Z_OFF], preferred_element_type=F32).T.astype(BF16)

    z = _gelu(jnp.dot(h, win_ref[:, Z_OFF:G_OFF], preferred_element_type=F32))
    u = z[:, :SGU_W]
    vv = z[:, SGU_W:]
    mu = jnp.mean(vv, axis=-1, keepdims=True)
    xc = vv - mu
    vln = (xc * lax.rsqrt(jnp.mean(xc * xc, axis=-1, keepdims=True) + LN_EPS) * lng_ref[...] + lnb_ref[...]).astype(BF16)
    nc = tm // SGU_CHUNK
    gw = SGU_W // SGU_GROUPS
    for g in range(SGU_GROUPS):
        rhs = jnp.concatenate([vln[n * SGU_CHUNK:(n + 1) * SGU_CHUNK, g * gw:(g + 1) * gw] for n in range(nc)], axis=1)
        mixed = jnp.dot(sw_ref[g], rhs, preferred_element_type=F32)
        for n in range(nc):
            rows = slice(n * SGU_CHUNK, (n + 1) * SGU_CHUNK)
            cols = slice(g * gw, (g + 1) * gw)
            s_scr[rows, cols] = (u[rows, cols] * (mixed[:, n * gw:(n + 1) * gw] + sb_ref[g])).astype(BF16)
    y_sgu = jnp.dot(s_scr[...], wbs_ref[...], preferred_element_type=F32)

    gate = jax.nn.sigmoid(jnp.dot(h, win_ref[:, G_OFF:G_OFF + 2 * D], preferred_element_type=F32) + bg_ref[...])
    ga_ref[0] = gate[:, :D].astype(BF16)
    t_ref[0] = (gate[:, D:] * y_sgu).astype(BF16)


def _inproj(x, mod, g_pre, win_b, b_gate, cos_t, sin_t, ln_g, ln_b, sw_b, sb_b, wbs_b, tm):
    B, L, _ = x.shape
    full = lambda shape: pl.BlockSpec(shape, lambda b, i: (0,) * len(shape))
    tile = pl.BlockSpec((1, tm, D), lambda b, i: (b, i, 0))
    tile_t = pl.BlockSpec((1, D, tm), lambda b, i: (b, 0, i))
    out = jax.ShapeDtypeStruct((B, L, D), BF16)
    out_t = jax.ShapeDtypeStruct((B, D, L), BF16)
    return pl.pallas_call(
        functools.partial(_inproj_kernel, tm=tm),
        grid=(B, L // tm),
        in_specs=[tile, full(mod.shape), full((1, D)), full(win_b.shape), full((1, 2 * D)),
                  pl.BlockSpec((tm, 128), lambda b, i: (i, 0)), pl.BlockSpec((tm, 128), lambda b, i: (i, 0)),
                  full((1, SGU_W)), full((1, SGU_W)), full(sw_b.shape), full(sb_b.shape), full(wbs_b.shape)],
        out_specs=[tile, tile, tile_t, tile, tile],
        out_shape=[out, out, out_t, out, out],
        scratch_shapes=[pltpu.VMEM((tm, SGU_W), BF16)],
        compiler_params=pltpu.CompilerParams(dimension_semantics=("arbitrary", "arbitrary"),
                                             vmem_limit_bytes=VMEM_LIMIT),
        name="inproj",
    )(x, mod, g_pre, win_b, b_gate, cos_t, sin_t, ln_g, ln_b, sw_b, sb_b, wbs_b)


def _ctxkv_kernel(ctx_ref, mod_ref, g_ref, wk_ref, wv_ref, k_ref, vt_ref, *, ctx_row):
    h = _prenorm(ctx_ref[0], g_ref, mod_ref, ctx_row, 0).astype(BF16)
    k_ref[0] = jnp.dot(h, wk_ref[...], preferred_element_type=F32).astype(BF16)
    vt_ref[0] = jnp.dot(h, wv_ref[...], preferred_element_type=F32).T.astype(BF16)


def _ctxkv(ctx, mod, g_pre, win_b, ctx_row):
    B, Lc, _ = ctx.shape
    tile = pl.BlockSpec((1, Lc, D), lambda b: (b, 0, 0))
    return pl.pallas_call(
        functools.partial(_ctxkv_kernel, ctx_row=ctx_row),
        grid=(B,),
        in_specs=[tile, pl.BlockSpec(mod.shape, lambda b: (0, 0)), pl.BlockSpec((1, D), lambda b: (0, 0)),
                  pl.BlockSpec((D, K_W), lambda b: (0, Q_W // K_W)),
                  pl.BlockSpec((D, V_W), lambda b: (0, (Q_W + K_W) // V_W))],
        out_specs=[tile, pl.BlockSpec((1, D, Lc), lambda b: (b, 0, 0))],
        out_shape=[jax.ShapeDtypeStruct((B, Lc, D), BF16), jax.ShapeDtypeStruct((B, D, Lc), BF16)],
        compiler_params=pltpu.CompilerParams(dimension_semantics=("arbitrary",), vmem_limit_bytes=VMEM_LIMIT),
        name="ctxkv",
    )(ctx, mod, g_pre, win_b, win_b)


ONES_ROWS = 16
KV_TILE_MAX = 4224


def _attn_kernel(q_ref, k_ref, vt_ref, lam_ref, sgt_ref, o_ref, s_scr, acc_scr, *, tq, nt, tk, nk, post_scale):
    lane = lax.broadcasted_iota(jnp.int32, (tq, HEAD_W), 1)
    ones = jnp.ones((ONES_ROWS, tk), BF16)

    def stacked_q(t):
        q = q_ref[0, t * tq:(t + 1) * tq, :]
        zero = jnp.zeros_like(q)
        return jnp.concatenate([jnp.where(lane < DH, q, zero), jnp.where(lane >= DH, q, zero)], axis=0)

    def scores(seq):
        t, j = divmod(seq, nk)
        kb = k_ref[0, j * tk:(j + 1) * tk, :]
        st = lax.dot_general(kb, stacked_q(t), (((1,), (1,)), ((), ())), preferred_element_type=F32)
        s_scr[seq % 2] = st
        return jnp.max(st, axis=0, keepdims=True)

    blk_max = scores(0)
    for seq in range(nt * nk):
        t, j = divmod(seq, nk)
        nxt_max = scores(seq + 1) if seq + 1 < nt * nk else None
        if j == 0:
            m = jnp.full((1, 2 * tq), NEG_BIG, F32)
        m_new = jnp.maximum(m, blk_max)
        pt = jnp.exp2(s_scr[seq % 2] - m_new).astype(BF16)
        vte = jnp.concatenate([vt_ref[0, :, j * tk:(j + 1) * tk], ones], axis=0)
        pv = jnp.dot(vte, pt, preferred_element_type=F32)
        acc_scr[...] = pv if j == 0 else jnp.exp2(m - m_new) * acc_scr[...] + pv
        m, blk_max = m_new, nxt_max
        if j == nk - 1:
            acc = acc_scr[...]
            o1 = acc[:HEAD_W, :tq] / acc[HEAD_W:HEAD_W + 1, :tq]
            o2 = acc[:HEAD_W, tq:] / acc[HEAD_W:HEAD_W + 1, tq:]
            ot = o1 - lam_ref[0:1, 0:tq] * o2
            on = ot * lax.rsqrt(jnp.mean(ot * ot, axis=0, keepdims=True) + RMS_EPS) * sgt_ref[...] * post_scale
            o_ref[0, t * tq:(t + 1) * tq, :] = on.T.astype(BF16)


def _kv_tile(lk):
    best = 128
    for t in range(128, KV_TILE_MAX + 1, 128):
        if lk % t == 0:
            best = t
    return best


def _attn(q, k_all, vt_all, lam, sg_t, lam_init, tq, nt):
    B, L, _ = q.shape
    Lk = k_all.shape[1]
    tk = _kv_tile(Lk)
    return pl.pallas_call(
        functools.partial(_attn_kernel, tq=tq, nt=nt, tk=tk, nk=Lk // tk, post_scale=1.0 - lam_init),
        grid=(B, HEADS, L // (nt * tq)),
        in_specs=[pl.BlockSpec((1, nt * tq, HEAD_W), lambda b, h, i: (b, i, h)),
                  pl.BlockSpec((1, Lk, HEAD_W), lambda b, h, i: (b, 0, h)),
                  pl.BlockSpec((1, HEAD_W, Lk), lambda b, h, i: (b, h, 0)),
                  pl.BlockSpec(lam.shape, lambda b, h, i: (0, 0)),
                  pl.BlockSpec((HEAD_W, tq), lambda b, h, i: (0, 0))],
        out_specs=pl.BlockSpec((1, nt * tq, HEAD_W), lambda b, h, i: (b, i, h)),
        out_shape=jax.ShapeDtypeStruct((B, L, D), BF16),
        scratch_shapes=[pltpu.VMEM((2, tk, 2 * tq), F32), pltpu.VMEM((HEAD_W + ONES_ROWS, 2 * tq), F32)],
        compiler_params=pltpu.CompilerParams(dimension_semantics=("arbitrary",) * 3, vmem_limit_bytes=VMEM_LIMIT),
        name="attn",
    )(q, k_all, vt_all, lam, sg_t)


def _mixout_kernel(o_ref, ga_ref, t_ref, x_ref, mod_ref, gpost_ref, gpre_ref, wba_ref, wout_ref, xn_ref, h2t_ref):
    b = pl.program_id(0)
    y_attn = jnp.dot(o_ref[0], wba_ref[...], preferred_element_type=F32)
    merged = (ga_ref[0].astype(F32) * y_attn + t_ref[0].astype(F32)).astype(BF16)
    y = jnp.dot(merged, wout_ref[...], preferred_element_type=F32)
    gt1 = mod_ref[pl.ds(b, 1), 2 * D:3 * D]
    x_new = x_ref[0] + gt1 * (_rms(y) * gpost_ref[...])
    xn_ref[0] = x_new
    h2 = _prenorm(x_new, gpre_ref, mod_ref, b, 3 * D)
    h2t_ref[0] = h2.T.astype(BF16)


def _mixout(o, ga, t, x, mod, g_post, g_pre_ffn, wba_b, wout_b, tm):
    B, L, _ = x.shape
    full = lambda shape: pl.BlockSpec(shape, lambda b, i: (0,) * len(shape))
    tile = pl.BlockSpec((1, tm, D), lambda b, i: (b, i, 0))
    return pl.pallas_call(
        _mixout_kernel,
        grid=(B, L // tm),
        in_specs=[tile, tile, tile, tile, full(mod.shape), full((1, D)), full((1, D)), full((D, D)), full((D, D))],
        out_specs=[tile, pl.BlockSpec((1, D, tm), lambda b, i: (b, 0, i))],
        out_shape=[jax.ShapeDtypeStruct((B, L, D), F32), jax.ShapeDtypeStruct((B, D, L), BF16)],
        compiler_params=pltpu.CompilerParams(dimension_semantics=("arbitrary", "arbitrary"),
                                             vmem_limit_bytes=VMEM_LIMIT),
        name="mixout",
    )(o, ga, t, x, mod, g_post, g_pre_ffn, wba_b, wout_b)


PEER_SUB = 4 * PEER_KEYS
PEER_TILE = (64, 128)
PEER_TOKEN_SPLIT = 1


def _sorting_network(n):
    pairs, p = [], 1
    while p < n:
        k = p
        while k >= 1:
            for j in range(k % p, n - k, 2 * k):
                for i in range(min(k, n - j - k)):
                    if (i + j) // (2 * p) == (i + j + k) // (2 * p):
                        pairs.append((i + j, i + j + k))
            k //= 2
        p *= 2
    return pairs


def _col_reduce8(x, op):
    for shift in (4, 2, 1):
        x = op(x, pltpu.roll(x, shift, 0))
    return x


def _slabs(x):
    return [x[8 * k:8 * (k + 1), :] for k in range(x.shape[0] // 8)]


def _top16_desc(s, top_scr, p):
    lists = _slabs(s)
    n = len(lists)
    for i, j in _sorting_network(n):
        lists[i], lists[j] = jnp.maximum(lists[i], lists[j]), jnp.minimum(lists[i], lists[j])
    for r in range(PEER_TOPK):
        m = _col_reduce8(lists[0], jnp.maximum)
        top_scr[p, r:r + 1, :] = m[0:1, :]
        if r + 1 < PEER_TOPK:
            hit = lists[0] == m
            for i in range(n - 1 - r):
                lists[i] = jnp.where(hit, lists[i + 1], lists[i])


def _peer_score(h2t, wq_ref, keys_ref, th_scr, t2_scr, r1_scr, top_scr, tt):
    qt = jnp.dot(wq_ref[...], h2t, preferred_element_type=F32)
    neg_inf = jnp.float32(-jnp.inf)
    rep = lambda p, k: jnp.broadcast_to(top_scr[p, k:k + 1, :], (8, tt))
    for h in range(PEER_HEADS):
        halves = []
        for p in range(2):
            r0 = (2 * h + p) * PEER_KEYS
            qhp = qt[r0:r0 + PEER_KEYS, :]
            qn = qhp * lax.rsqrt(jnp.mean(qhp * qhp, axis=0, keepdims=True) + RMS_EPS)
            s = jnp.dot(keys_ref[p], qn.astype(BF16), preferred_element_type=F32)
            halves.append(_slabs(s))
            _top16_desc(s, top_scr, p)
        s1, s2 = halves
        a1_lo, a1_hi = top_scr[0, 0:8, :], top_scr[0, 8:16, :]
        a2_lo, a2_hi = top_scr[1, 0:8, :], top_scr[1, 8:16, :]
        cands = [rep(0, 0) + a2_lo, rep(0, 0) + a2_hi, a1_hi + rep(1, 0)]
        cands += [rep(0, j) + a2_lo for j in range(1, 8)]
        work = list(cands)
        tau = None
        for r in range(PEER_TOPK):
            m = _col_reduce8(functools.reduce(jnp.maximum, work), jnp.maximum)
            if r + 1 < PEER_TOPK:
                work = [jnp.where(c == m, neg_inf, c) for c in work]
            else:
                tau = m
        cmax = rep(0, 0) + rep(1, 0)
        zsum = _col_reduce8(
            functools.reduce(lambda a, c: a + c, [jnp.where(c >= tau, jnp.exp(c - cmax), 0.0) for c in cands]),
            lambda a, c: a + c)
        theta = [jnp.full((8, tt), jnp.inf, F32)] * len(s1)
        for k in range(PEER_TOPK):
            a2k = rep(1, k)
            passing = jnp.minimum(jnp.where(a1_lo + a2k >= tau, a1_lo, jnp.inf),
                                  jnp.where(a1_hi + a2k >= tau, a1_hi, jnp.inf))
            bound = _col_reduce8(passing, jnp.minimum)
            theta = [jnp.where(s >= bound, a2k, t) for s, t in zip(s1, theta)]
        lz = jnp.log2(zsum)
        a1_max, a2_max = rep(0, 0), rep(1, 0)
        lz = lz + 1.0
        for i in range(len(s1)):
            rows = slice(8 * i, 8 * (i + 1))
            th_scr[h, rows, :] = (theta[i] - a2_max) * LOG2E - lz
            t2_scr[h, rows, :] = (s2[i] - a2_max) * LOG2E - lz
            r1_scr[h, rows, :] = (s1[i] - a1_max) * LOG2E


def _peer_kernel(h2t_ref, xn_ref, mod_ref, gpost_ref, wq_ref, keys_ref, u_ref, vt_ref, out_ref,
                 th_scr, t2_scr, r1_scr, top_scr, acc_scr, a_scr, w_scr, thc_scr, r1c_scr, *, tt, ch):
    b = pl.program_id(0)
    c = pl.program_id(2)

    @pl.when(c == 0)
    def _():
        _peer_score(h2t_ref[0], wq_ref, keys_ref, th_scr, t2_scr, r1_scr, top_scr, tt)
        acc_scr[...] = jnp.zeros(acc_scr.shape, F32)

    tr, tl = PEER_TILE
    nj = PEER_SUB // PEER_KEYS
    nsub = ch // PEER_SUB
    step_rows = pl.ds(pl.multiple_of(c * (ch // PEER_KEYS), 8), ch // PEER_KEYS)
    thc_scr[...] = th_scr[:, step_rows, :]
    r1c_scr[...] = r1_scr[:, step_rows, :]

    tw = tt // PEER_TOKEN_SPLIT
    stages = [(sc, hf) for sc in range(nsub) for hf in range(PEER_TOKEN_SPLIT)]

    def experts(sc):
        return slice(sc * PEER_SUB, (sc + 1) * PEER_SUB)

    def tokens(hf):
        return slice(hf * tw, (hf + 1) * tw)

    def activations(s):
        sc, hf = stages[s]
        x = jnp.dot(u_ref[experts(sc), :], h2t_ref[0, :, tokens(hf)], preferred_element_type=F32)
        a_scr[s % 2] = x * (1.0 + lax.erf(x * INV_SQRT2))

    def mix_out(s):
        sc, hf = stages[s]
        acc_scr[:, tokens(hf)] += jnp.dot(vt_ref[:, experts(sc)], w_scr[s % 2], preferred_element_type=F32)

    activations(0)
    for s, (sc, hf) in enumerate(stages):
        if s + 1 < len(stages):
            activations(s + 1)
        if s > 0:
            mix_out(s - 1)
        for rt in range(PEER_KEYS // tr):
            for lt in range(tw // tl):
                r = slice(rt * tr, (rt + 1) * tr)
                l = slice(hf * tw + lt * tl, hf * tw + (lt + 1) * tl)
                ls = slice(lt * tl, (lt + 1) * tl)
                g = [None] * nj
                for h in range(PEER_HEADS):
                    t2t = t2_scr[h, r, l]
                    for j in range(nj):
                        row = slice(sc * nj + j, sc * nj + j + 1)
                        gate = jnp.where(t2t >= thc_scr[h, row, l], jnp.exp2(t2t + r1c_scr[h, row, l]), 0.0)
                        g[j] = gate if h == 0 else g[j] + gate
                for j in range(nj):
                    rows = slice(j * PEER_KEYS + rt * tr, j * PEER_KEYS + (rt + 1) * tr)
                    w_scr[s % 2, rows, ls] = (a_scr[s % 2, rows, ls] * g[j]).astype(BF16)
    mix_out(len(stages) - 1)

    @pl.when(c == pl.num_programs(2) - 1)
    def _():
        y = acc_scr[...].T
        gt2 = mod_ref[pl.ds(b, 1), 5 * D:6 * D]
        out_ref[0] = xn_ref[0] + gt2 * (_rms(y) * gpost_ref[...])


def _peer(h2t, x_new, mod, g_post, wq_t, keys_b, u_b, vt_b, tt, ch):
    B, _, L = h2t.shape
    full = lambda shape: pl.BlockSpec(shape, lambda b, i, c: (0,) * len(shape))
    tile = pl.BlockSpec((1, tt, D), lambda b, i, c: (b, i, 0))
    per_head = pltpu.VMEM((PEER_HEADS, PEER_KEYS, tt), F32)
    nsub = ch // PEER_SUB
    return pl.pallas_call(
        functools.partial(_peer_kernel, tt=tt, ch=ch),
        grid=(B, L // tt, PEER_EXPERTS // ch),
        in_specs=[pl.BlockSpec((1, D, tt), lambda b, i, c: (b, 0, i)), tile, full(mod.shape), full((1, D)),
                  full(wq_t.shape), full(keys_b.shape),
                  pl.BlockSpec((ch, D), lambda b, i, c: (c, 0)),
                  pl.BlockSpec((D, ch), lambda b, i, c: (0, c))],
        out_specs=tile,
        out_shape=jax.ShapeDtypeStruct((B, L, D), F32),
        scratch_shapes=[per_head, per_head, per_head, pltpu.VMEM((2, PEER_TOPK, tt), F32),
                        pltpu.VMEM((D, tt), F32), pltpu.VMEM((2, PEER_SUB, tt // PEER_TOKEN_SPLIT), F32),
                        pltpu.VMEM((2, PEER_SUB, tt // PEER_TOKEN_SPLIT), BF16),
                        pltpu.VMEM((PEER_HEADS, ch // PEER_KEYS, tt), F32),
                        pltpu.VMEM((PEER_HEADS, ch // PEER_KEYS, tt), F32)],
        compiler_params=pltpu.CompilerParams(dimension_semantics=("arbitrary",) * 3, vmem_limit_bytes=VMEM_LIMIT),
        name="peer",
    )(h2t, x_new, mod, g_post, wq_t, keys_b, u_b, vt_b)


def _rope_tables(L):
    nfreq = DH // 4
    inv = ROPE_BASE ** (-jnp.arange(nfreq, dtype=F32) / nfreq)
    t = jnp.arange(L, dtype=jnp.int32)
    rows = (t // GRID_W).astype(F32)[:, None] * inv[None, :]
    cols = (t % GRID_W).astype(F32)[:, None] * inv[None, :]
    cos64 = jnp.concatenate([jnp.cos(rows), jnp.cos(rows), jnp.cos(cols), jnp.cos(cols)], axis=1)
    sin64 = jnp.concatenate([-jnp.sin(rows), jnp.sin(rows), -jnp.sin(cols), jnp.sin(cols)], axis=1)
    return jnp.tile(cos64, (1, 2)), jnp.tile(sin64, (1, 2))


def _tile_plan(L):
    token_tile = 512
    query_tile = 512
    per_step = max(n for n in (1, 2, 4) if L % (n * query_tile) == 0)
    plan = dict(token_tile=token_tile, query_tile=query_tile, query_tiles_per_step=per_step,
                peer_token_tile=512, peer_experts_per_step=2048)
    assert L % token_tile == 0 and L % plan["peer_token_tile"] == 0 and L % SGU_CHUNK == 0
    return plan


def kernel(x, c, ctx, c_ctx, w_ada, b_ada, g_pre_mix, g_post_mix, g_pre_ffn, g_post_ffn, w_in, b_gate, lambda_q1, lambda_k1, lambda_q2, lambda_k2, subln_g, sgu_ln_g, sgu_ln_b, sgu_w, sgu_b, w_branch_attn, w_branch_sgu, w_out, peer_w_query, peer_sub_keys, peer_u, peer_v):
    B, L, _ = x.shape
    depth = w_ada.shape[0]
    assert depth == 1 and B <= 7
    lam_init = 0.8 - 0.6 * math.exp(-0.3 * 0)
    tiles = _tile_plan(L)
    tm, tq = tiles["token_tile"], tiles["query_tile"]

    row = lambda a: a[0].reshape(1, -1)
    cc = jnp.concatenate([c, c_ctx[None, :], jnp.zeros((8 - B - 1, D), F32)], axis=0)
    mod, lam = _ada(cc, w_ada[0], row(b_ada), row(lambda_q1), row(lambda_k1), row(lambda_q2), row(lambda_k2), lam_init)

    win_b = w_in[0].astype(BF16)
    cos_t, sin_t = _rope_tables(L)
    sb_b = jnp.broadcast_to(sgu_b[0][:, :, None], (SGU_GROUPS, SGU_CHUNK, SGU_W // SGU_GROUPS))
    q, k, vt, t_sgu, g_attn = _inproj(x, mod, row(g_pre_mix), win_b, row(b_gate), cos_t, sin_t,
                                      row(sgu_ln_g), row(sgu_ln_b), sgu_w[0].astype(BF16), sb_b,
                                      w_branch_sgu[0].astype(BF16), tm)
    kc, vct = _ctxkv(ctx, mod, row(g_pre_mix), win_b, B)
    k_all = jnp.concatenate([kc, k], axis=1)
    vt_all = jnp.concatenate([vct, vt], axis=2)
    sg_t = jnp.broadcast_to(subln_g[0][:, None], (HEAD_W, tq))
    o = _attn(q, k_all, vt_all, lam, sg_t, lam_init, tq, tiles["query_tiles_per_step"])

    x_new, h2t = _mixout(o, g_attn, t_sgu, x, mod, row(g_post_mix), row(g_pre_ffn),
                         w_branch_attn[0].astype(BF16), w_out[0].astype(BF16), tm)
    return _peer(h2t, x_new, mod, row(g_post_ffn), peer_w_query[0].T.astype(BF16),
                 peer_sub_keys[0].astype(BF16), peer_u[0].astype(BF16), peer_v[0].T.astype(BF16),
                 tiles["peer_token_tile"], tiles["peer_experts_per_step"])
```

```python
import functools
import math

import jax
import jax.numpy as jnp
from jax import lax
from jax.experimental import pallas as pl
from jax.experimental.pallas import tpu as pltpu

F32 = jnp.float32
BF16 = jnp.bfloat16

D = 1024
N_ADA = 6
GRID_W = 64
HEADS = 8
DH = 64
HEAD_W = 2 * DH
ROPE_BASE = 10000.0
SGU_CHUNK = 128
SGU_GROUPS = 8
SGU_W = 1024
Q_W = K_W = V_W = 1024
Z_W = 2 * SGU_W
Z_OFF = Q_W + K_W + V_W
G_OFF = Z_OFF + Z_W
PEER_HEADS = 8
PEER_KEYS = 128
PEER_TOPK = 16
PEER_EXPERTS = PEER_KEYS * PEER_KEYS
RMS_EPS = 1e-6
LN_EPS = 1e-5
LOG2E = 1.4426950408889634
INV_SQRT2 = 0.7071067811865476
NEG_BIG = -1e30
LAM_W = 512

VMEM_LIMIT = 56 * 1024 * 1024


def _gelu(x):
    return 0.5 * x * (1.0 + lax.erf(x * INV_SQRT2))


def _rms(x, eps=RMS_EPS):
    return x * lax.rsqrt(jnp.mean(x * x, axis=-1, keepdims=True) + eps)


def _ada_kernel(cc_ref, w_ref, b_ref, lq1_ref, lk1_ref, lq2_ref, lk2_ref, mod_ref, lam_ref, *, lam_init):
    a = cc_ref[...]
    s = (a * jax.nn.sigmoid(a)).astype(BF16)
    mod_ref[...] = jnp.dot(s, w_ref[...].astype(BF16), preferred_element_type=F32) + b_ref[...]
    d1 = jnp.sum(lq1_ref[...] * lk1_ref[...], axis=-1, keepdims=True)
    d2 = jnp.sum(lq2_ref[...] * lk2_ref[...], axis=-1, keepdims=True)
    lam = jnp.exp(d1) - jnp.exp(d2) + lam_init
    lam_ref[...] = jnp.broadcast_to(lam, lam_ref.shape)


def _ada(cc, w_ada, b_ada, lq1, lk1, lq2, lk2, lam_init):
    tn = 1536
    n = w_ada.shape[1]
    vec = pl.BlockSpec((1, DH), lambda j: (0, 0))
    return pl.pallas_call(
        functools.partial(_ada_kernel, lam_init=lam_init),
        grid=(n // tn,),
        in_specs=[pl.BlockSpec((8, D), lambda j: (0, 0)),
                  pl.BlockSpec((D, tn), lambda j: (0, j)),
                  pl.BlockSpec((1, tn), lambda j: (0, j)),
                  vec, vec, vec, vec],
        out_specs=[pl.BlockSpec((8, tn), lambda j: (0, j)),
                   pl.BlockSpec((8, LAM_W), lambda j: (0, 0))],
        out_shape=[jax.ShapeDtypeStruct((8, n), F32), jax.ShapeDtypeStruct((8, LAM_W), F32)],
        compiler_params=pltpu.CompilerParams(dimension_semantics=("arbitrary",), vmem_limit_bytes=VMEM_LIMIT),
        name="ada",
    )(cc, w_ada, b_ada, lq1, lk1, lq2, lk2)


def _prenorm(x, g_ref, mod_ref, row, col0):
    sh = mod_ref[pl.ds(row, 1), col0:col0 + D]
    sc = mod_ref[pl.ds(row, 1), col0 + D:col0 + 2 * D]
    return _rms(x) * g_ref[...] * (1.0 + sc) + sh


def _inproj_kernel(x_ref, mod_ref, g_ref, win_ref, bg_ref, cos_ref, sin_ref, lng_ref, lnb_ref,
                   sw_ref, sb_ref, wbs_ref, q_ref, k_ref, vt_ref, t_ref, ga_ref, s_scr, *, tm):
    b = pl.program_id(0)
    h = _prenorm(x_ref[0], g_ref, mod_ref, b, 0).astype(BF16)

    lane = lax.broadcasted_iota(jnp.int32, (tm, 128), 1)
    first = (lane & 16) == 0
    cosv = cos_ref[...]
    sinv = sin_ref[...]

    def rope(t):
        swapped = jnp.where(first, pltpu.roll(t, 112, 1), pltpu.roll(t, 16, 1))
        return t * cosv + swapped * sinv

    q = jnp.dot(h, win_ref[:, 0:Q_W], preferred_element_type=F32)
    for j in range(Q_W // 128):
        q_ref[0, :, j * 128:(j + 1) * 128] = (rope(q[:, j * 128:(j + 1) * 128]) * (DH ** -0.5 * LOG2E)).astype(BF16)
    k = jnp.dot(h, win_ref[:, Q_W:Q_W + K_W], preferred_element_type=F32)
    for j in range(K_W // 128):
        k_ref[0, :, j * 128:(j + 1) * 128] = rope(k[:, j * 128:(j + 1) * 128]).astype(BF16)
    vt_ref[0] = jnp.dot(h, win_ref[:, Q_W + K_W:Z_OFF], preferred_element_type=F32).T.astype(BF16)

    z = _gelu(jnp.dot(h, win_ref[:, Z_OFF:G_OFF], preferred_element_type=F32))
    u = z[:, :SGU_W]
    vv = z[:, SGU_W:]
    mu = jnp.mean(vv, axis=-1, keepdims=True)
    xc = vv - mu
    vln = (xc * lax.rsqrt(jnp.mean(xc * xc, axis=-1, keepdims=True) + LN_EPS) * lng_ref[...] + lnb_ref[...]).astype(BF16)
    nc = tm // SGU_CHUNK
    gw = SGU_W // SGU_GROUPS
    for g in range(SGU_GROUPS):
        rhs = jnp.concatenate([vln[n * SGU_CHUNK:(n + 1) * SGU_CHUNK, g * gw:(g + 1) * gw] for n in range(nc)], axis=1)
        mixed = jnp.dot(sw_ref[g], rhs, preferred_element_type=F32)
        for n in range(nc):
            rows = slice(n * SGU_CHUNK, (n + 1) * SGU_CHUNK)
            cols = slice(g * gw, (g + 1) * gw)
            s_scr[rows, cols] = (u[rows, cols] * (mixed[:, n * gw:(n + 1) * gw] + sb_ref[g])).astype(BF16)
    y_sgu = jnp.dot(s_scr[...], wbs_ref[...], preferred_element_type=F32)

    gate = jax.nn.sigmoid(jnp.dot(h, win_ref[:, G_OFF:G_OFF + 2 * D], preferred_element_type=F32) + bg_ref[...])
    ga_ref[0] = gate[:, :D].astype(BF16)
    t_ref[0] = (gate[:, D:] * y_sgu).astype(BF16)


def _inproj(x, mod, g_pre, win_b, b_gate, cos_t, sin_t, ln_g, ln_b, sw_b, sb_b, wbs_b, tm):
    B, L, _ = x.shape
    full = lambda shape: pl.BlockSpec(shape, lambda b, i: (0,) * len(shape))
    tile = pl.BlockSpec((1, tm, D), lambda b, i: (b, i, 0))
    tile_t = pl.BlockSpec((1, D, tm), lambda b, i: (b, 0, i))
    out = jax.ShapeDtypeStruct((B, L, D), BF16)
    out_t = jax.ShapeDtypeStruct((B, D, L), BF16)
    return pl.pallas_call(
        functools.partial(_inproj_kernel, tm=tm),
        grid=(B, L // tm),
        in_specs=[tile, full(mod.shape), full((1, D)), full(win_b.shape), full((1, 2 * D)),
                  pl.BlockSpec((tm, 128), lambda b, i: (i, 0)), pl.BlockSpec((tm, 128), lambda b, i: (i, 0)),
                  full((1, SGU_W)), full((1, SGU_W)), full(sw_b.shape), full(sb_b.shape), full(wbs_b.shape)],
        out_specs=[tile, tile, tile_t, tile, tile],
        out_shape=[out, out, out_t, out, out],
        scratch_shapes=[pltpu.VMEM((tm, SGU_W), BF16)],
        compiler_params=pltpu.CompilerParams(dimension_semantics=("arbitrary", "arbitrary"),
                                             vmem_limit_bytes=VMEM_LIMIT),
        name="inproj",
    )(x, mod, g_pre, win_b, b_gate, cos_t, sin_t, ln_g, ln_b, sw_b, sb_b, wbs_b)


def _ctxkv_kernel(ctx_ref, mod_ref, g_ref, wk_ref, wv_ref, k_ref, vt_ref, *, ctx_row):
    h = _prenorm(ctx_ref[0], g_ref, mod_ref, ctx_row, 0).astype(BF16)
    k_ref[0] = jnp.dot(h, wk_ref[...], preferred_element_type=F32).astype(BF16)
    vt_ref[0] = jnp.dot(h, wv_ref[...], preferred_element_type=F32).T.astype(BF16)


def _ctxkv(ctx, mod, g_pre, win_b, ctx_row):
    B, Lc, _ = ctx.shape
    tile = pl.BlockSpec((1, Lc, D), lambda b: (b, 0, 0))
    return pl.pallas_call(
        functools.partial(_ctxkv_kernel, ctx_row=ctx_row),
        grid=(B,),
        in_specs=[tile, pl.BlockSpec(mod.shape, lambda b: (0, 0)), pl.BlockSpec((1, D), lambda b: (0, 0)),
                  pl.BlockSpec((D, K_W), lambda b: (0, Q_W // K_W)),
                  pl.BlockSpec((D, V_W), lambda b: (0, (Q_W + K_W) // V_W))],
        out_specs=[tile, pl.BlockSpec((1, D, Lc), lambda b: (b, 0, 0))],
        out_shape=[jax.ShapeDtypeStruct((B, Lc, D), BF16), jax.ShapeDtypeStruct((B, D, Lc), BF16)],
        compiler_params=pltpu.CompilerParams(dimension_semantics=("arbitrary",), vmem_limit_bytes=VMEM_LIMIT),
        name="ctxkv",
    )(ctx, mod, g_pre, win_b, win_b)


ONES_ROWS = 16
KV_BLOCK = 1536
KV_TILE_MAX = 4224


def _attn_kernel(q_ref, k_ref, vt_ref, lam_ref, sgt_ref, o_ref, s_scr, acc_scr, *, tq, nt, tk, nk, post_scale):
    lane = lax.broadcasted_iota(jnp.int32, (tq, HEAD_W), 1)
    ones = jnp.ones((ONES_ROWS, tk), BF16)

    def stacked_q(t):
        q = q_ref[0, t * tq:(t + 1) * tq, :]
        zero = jnp.zeros_like(q)
        return jnp.concatenate([jnp.where(lane < DH, q, zero), jnp.where(lane >= DH, q, zero)], axis=0)

    def scores(seq):
        t, j = divmod(seq, nk)
        kb = k_ref[0, j * tk:(j + 1) * tk, :]
        st = lax.dot_general(kb, stacked_q(t), (((1,), (1,)), ((), ())), preferred_element_type=F32)
        s_scr[seq % 2] = st
        return jnp.max(st, axis=0, keepdims=True)

    blk_max = scores(0)
    for seq in range(nt * nk):
        t, j = divmod(seq, nk)
        nxt_max = scores(seq + 1) if seq + 1 < nt * nk else None
        if j == 0:
            m = jnp.full((1, 2 * tq), NEG_BIG, F32)
        m_new = jnp.maximum(m, blk_max)
        pv = None
        for r0 in range(0, tk, KV_BLOCK):
            rows = slice(r0, min(r0 + KV_BLOCK, tk))
            pt = jnp.exp2(s_scr[seq % 2, rows, :] - m_new).astype(BF16)
            vte = jnp.concatenate([vt_ref[0, :, j * tk + rows.start:j * tk + rows.stop], ones[:, rows]], axis=0)
            part = jnp.dot(vte, pt, preferred_element_type=F32)
            pv = part if pv is None else pv + part
        acc_scr[...] = pv if j == 0 else jnp.exp2(m - m_new) * acc_scr[...] + pv
        m, blk_max = m_new, nxt_max
        if j == nk - 1:
            acc = acc_scr[...]
            o1 = acc[:HEAD_W, :tq] / acc[HEAD_W:HEAD_W + 1, :tq]
            o2 = acc[:HEAD_W, tq:] / acc[HEAD_W:HEAD_W + 1, tq:]
            ot = o1 - lam_ref[0:1, 0:tq] * o2
            on = ot * lax.rsqrt(jnp.mean(ot * ot, axis=0, keepdims=True) + RMS_EPS) * sgt_ref[...] * post_scale
            o_ref[0, t * tq:(t + 1) * tq, :] = on.T.astype(BF16)


def _kv_tile(lk):
    best = 128
    for t in range(128, KV_TILE_MAX + 1, 128):
        if lk % t == 0:
            best = t
    return best


def _attn(q, k_all, vt_all, lam, sg_t, lam_init, tq, nt):
    B, L, _ = q.shape
    Lk = k_all.shape[1]
    tk = _kv_tile(Lk)
    return pl.pallas_call(
        functools.partial(_attn_kernel, tq=tq, nt=nt, tk=tk, nk=Lk // tk, post_scale=1.0 - lam_init),
        grid=(B, HEADS, L // (nt * tq)),
        in_specs=[pl.BlockSpec((1, nt * tq, HEAD_W), lambda b, h, i: (b, i, h)),
                  pl.BlockSpec((1, Lk, HEAD_W), lambda b, h, i: (b, 0, h)),
                  pl.BlockSpec((1, HEAD_W, Lk), lambda b, h, i: (b, h, 0)),
                  pl.BlockSpec(lam.shape, lambda b, h, i: (0, 0)),
                  pl.BlockSpec((HEAD_W, tq), lambda b, h, i: (0, 0))],
        out_specs=pl.BlockSpec((1, nt * tq, HEAD_W), lambda b, h, i: (b, i, h)),
        out_shape=jax.ShapeDtypeStruct((B, L, D), BF16),
        scratch_shapes=[pltpu.VMEM((2, tk, 2 * tq), F32), pltpu.VMEM((HEAD_W + ONES_ROWS, 2 * tq), F32)],
        compiler_params=pltpu.CompilerParams(dimension_semantics=("arbitrary",) * 3, vmem_limit_bytes=VMEM_LIMIT),
        name="attn",
    )(q, k_all, vt_all, lam, sg_t)


def _mixout_kernel(o_ref, ga_ref, t_ref, x_ref, mod_ref, gpost_ref, gpre_ref, wba_ref, wout_ref, xn_ref, h2t_ref):
    b = pl.program_id(0)
    y_attn = jnp.dot(o_ref[0], wba_ref[...], preferred_element_type=F32)
    merged = (ga_ref[0].astype(F32) * y_attn + t_ref[0].astype(F32)).astype(BF16)
    y = jnp.dot(merged, wout_ref[...], preferred_element_type=F32)
    gt1 = mod_ref[pl.ds(b, 1), 2 * D:3 * D]
    x_new = x_ref[0] + gt1 * (_rms(y) * gpost_ref[...])
    xn_ref[0] = x_new
    h2 = _prenorm(x_new, gpre_ref, mod_ref, b, 3 * D)
    h2t_ref[0] = h2.T.astype(BF16)


def _mixout(o, ga, t, x, mod, g_post, g_pre_ffn, wba_b, wout_b, tm):
    B, L, _ = x.shape
    full = lambda shape: pl.BlockSpec(shape, lambda b, i: (0,) * len(shape))
    tile = pl.BlockSpec((1, tm, D), lambda b, i: (b, i, 0))
    return pl.pallas_call(
        _mixout_kernel,
        grid=(B, L // tm),
        in_specs=[tile, tile, tile, tile, full(mod.shape), full((1, D)), full((1, D)), full((D, D)), full((D, D))],
        out_specs=[tile, pl.BlockSpec((1, D, tm), lambda b, i: (b, 0, i))],
        out_shape=[jax.ShapeDtypeStruct((B, L, D), F32), jax.ShapeDtypeStruct((B, D, L), BF16)],
        compiler_params=pltpu.CompilerParams(dimension_semantics=("arbitrary", "arbitrary"),
                                             vmem_limit_bytes=VMEM_LIMIT),
        name="mixout",
    )(o, ga, t, x, mod, g_post, g_pre_ffn, wba_b, wout_b)


PEER_SUB = 4 * PEER_KEYS
PEER_TILE = (64, 128)
PEER_TOKEN_SPLIT = 1


def _sorting_network(n):
    pairs, p = [], 1
    while p < n:
        k = p
        while k >= 1:
            for j in range(k % p, n - k, 2 * k):
                for i in range(min(k, n - j - k)):
                    if (i + j) // (2 * p) == (i + j + k) // (2 * p):
                        pairs.append((i + j, i + j + k))
            k //= 2
        p *= 2
    return pairs


def _col_reduce8(x, op):
    for shift in (4, 2, 1):
        x = op(x, pltpu.roll(x, shift, 0))
    return x


def _slabs(x):
    return [x[8 * k:8 * (k + 1), :] for k in range(x.shape[0] // 8)]


def _top16_desc(s, top_scr, p):
    lists = _slabs(s)
    n = len(lists)
    for i, j in _sorting_network(n):
        lists[i], lists[j] = jnp.maximum(lists[i], lists[j]), jnp.minimum(lists[i], lists[j])
    for r in range(PEER_TOPK):
        m = _col_reduce8(lists[0], jnp.maximum)
        top_scr[p, r:r + 1, :] = m[0:1, :]
        if r + 1 < PEER_TOPK:
            hit = lists[0] == m
            for i in range(n - 1 - r):
                lists[i] = jnp.where(hit, lists[i + 1], lists[i])


def _peer_score(h2t, wq_ref, keys_ref, th_scr, t2_scr, r1_scr, top_scr, tt):
    qt = jnp.dot(wq_ref[...], h2t, preferred_element_type=F32)
    neg_inf = jnp.float32(-jnp.inf)
    rep = lambda p, k: jnp.broadcast_to(top_scr[p, k:k + 1, :], (8, tt))
    for h in range(PEER_HEADS):
        halves = []
        for p in range(2):
            r0 = (2 * h + p) * PEER_KEYS
            qhp = qt[r0:r0 + PEER_KEYS, :]
            qn = qhp * lax.rsqrt(jnp.mean(qhp * qhp, axis=0, keepdims=True) + RMS_EPS)
            s = jnp.dot(keys_ref[p], qn.astype(BF16), preferred_element_type=F32)
            halves.append(_slabs(s))
            _top16_desc(s, top_scr, p)
        s1, s2 = halves
        a1_lo, a1_hi = top_scr[0, 0:8, :], top_scr[0, 8:16, :]
        a2_lo, a2_hi = top_scr[1, 0:8, :], top_scr[1, 8:16, :]
        cands = [rep(0, 0) + a2_lo, rep(0, 0) + a2_hi, a1_hi + rep(1, 0)]
        cands += [rep(0, j) + a2_lo for j in range(1, 8)]
        work = list(cands)
        tau = None
        for r in range(PEER_TOPK):
            m = _col_reduce8(functools.reduce(jnp.maximum, work), jnp.maximum)
            if r + 1 < PEER_TOPK:
                work = [jnp.where(c == m, neg_inf, c) for c in work]
            else:
                tau = m
        cmax = rep(0, 0) + rep(1, 0)
        zsum = _col_reduce8(
            functools.reduce(lambda a, c: a + c, [jnp.where(c >= tau, jnp.exp(c - cmax), 0.0) for c in cands]),
            lambda a, c: a + c)
        theta = [jnp.full((8, tt), jnp.inf, F32)] * len(s1)
        for k in range(PEER_TOPK):
            a2k = rep(1, k)
            passing = jnp.minimum(jnp.where(a1_lo + a2k >= tau, a1_lo, jnp.inf),
                                  jnp.where(a1_hi + a2k >= tau, a1_hi, jnp.inf))
            bound = _col_reduce8(passing, jnp.minimum)
            theta = [jnp.where(s >= bound, a2k, t) for s, t in zip(s1, theta)]
        lz = jnp.log2(zsum)
        a1_max, a2_max = rep(0, 0), rep(1, 0)
        lz = lz + 1.0
        for i in range(len(s1)):
            rows = slice(8 * i, 8 * (i + 1))
            th_scr[h, rows, :] = (theta[i] - a2_max) * LOG2E - lz
            t2_scr[h, rows, :] = (s2[i] - a2_max) * LOG2E - lz
            r1_scr[h, rows, :] = (s1[i] - a1_max) * LOG2E


def _peer_kernel(h2t_ref, xn_ref, mod_ref, gpost_ref, wq_ref, keys_ref, u_ref, vt_ref, out_ref,
                 th_scr, t2_scr, r1_scr, top_scr, acc_scr, a_scr, w_scr, thc_scr, r1c_scr, *, tt, ch):
    b = pl.program_id(0)
    c = pl.program_id(2)

    @pl.when(c == 0)
    def _():
        _peer_score(h2t_ref[0], wq_ref, keys_ref, th_scr, t2_scr, r1_scr, top_scr, tt)
        acc_scr[...] = jnp.zeros(acc_scr.shape, F32)

    tr, tl = PEER_TILE
    nj = PEER_SUB // PEER_KEYS
    nsub = ch // PEER_SUB
    step_rows = pl.ds(pl.multiple_of(c * (ch // PEER_KEYS), 8), ch // PEER_KEYS)
    thc_scr[...] = th_scr[:, step_rows, :]
    r1c_scr[...] = r1_scr[:, step_rows, :]

    tw = tt // PEER_TOKEN_SPLIT
    stages = [(sc, hf) for sc in range(nsub) for hf in range(PEER_TOKEN_SPLIT)]

    def experts(sc):
        return slice(sc * PEER_SUB, (sc + 1) * PEER_SUB)

    def tokens(hf):
        return slice(hf * tw, (hf + 1) * tw)

    def activations(s):
        sc, hf = stages[s]
        x = jnp.dot(u_ref[experts(sc), :], h2t_ref[0, :, tokens(hf)], preferred_element_type=F32)
        a_scr[s % 2] = x * (1.0 + lax.erf(x * INV_SQRT2))

    def mix_out(s):
        sc, hf = stages[s]
        acc_scr[:, tokens(hf)] += jnp.dot(vt_ref[:, experts(sc)], w_scr[s % 2], preferred_element_type=F32)

    activations(0)
    for s, (sc, hf) in enumerate(stages):
        if s + 1 < len(stages):
            activations(s + 1)
        if s > 0:
            mix_out(s - 1)
        for rt in range(PEER_KEYS // tr):
            for lt in range(tw // tl):
                r = slice(rt * tr, (rt + 1) * tr)
                l = slice(hf * tw + lt * tl, hf * tw + (lt + 1) * tl)
                ls = slice(lt * tl, (lt + 1) * tl)
                g = [None] * nj
                for h in range(PEER_HEADS):
                    t2t = t2_scr[h, r, l]
                    for j in range(nj):
                        row = slice(sc * nj + j, sc * nj + j + 1)
                        gate = jnp.where(t2t >= thc_scr[h, row, l], jnp.exp2(t2t + r1c_scr[h, row, l]), 0.0)
                        g[j] = gate if h == 0 else g[j] + gate
                for j in range(nj):
                    rows = slice(j * PEER_KEYS + rt * tr, j * PEER_KEYS + (rt + 1) * tr)
                    w_scr[s % 2, rows, ls] = (a_scr[s % 2, rows, ls] * g[j]).astype(BF16)
    mix_out(len(stages) - 1)

    @pl.when(c == pl.num_programs(2) - 1)
    def _():
        y = acc_scr[...].T
        gt2 = mod_ref[pl.ds(b, 1), 5 * D:6 * D]
        out_ref[0] = xn_ref[0] + gt2 * (_rms(y) * gpost_ref[...])


def _peer(h2t, x_new, mod, g_post, wq_t, keys_b, u_b, vt_b, tt, ch):
    B, _, L = h2t.shape
    full = lambda shape: pl.BlockSpec(shape, lambda b, i, c: (0,) * len(shape))
    tile = pl.BlockSpec((1, tt, D), lambda b, i, c: (b, i, 0))
    per_head = pltpu.VMEM((PEER_HEADS, PEER_KEYS, tt), F32)
    nsub = ch // PEER_SUB
    return pl.pallas_call(
        functools.partial(_peer_kernel, tt=tt, ch=ch),
        grid=(B, L // tt, PEER_EXPERTS // ch),
        in_specs=[pl.BlockSpec((1, D, tt), lambda b, i, c: (b, 0, i)), tile, full(mod.shape), full((1, D)),
                  full(wq_t.shape), full(keys_b.shape),
                  pl.BlockSpec((ch, D), lambda b, i, c: (c, 0)),
                  pl.BlockSpec((D, ch), lambda b, i, c: (0, c))],
        out_specs=tile,
        out_shape=jax.ShapeDtypeStruct((B, L, D), F32),
        scratch_shapes=[per_head, per_head, per_head, pltpu.VMEM((2, PEER_TOPK, tt), F32),
                        pltpu.VMEM((D, tt), F32), pltpu.VMEM((2, PEER_SUB, tt // PEER_TOKEN_SPLIT), F32),
                        pltpu.VMEM((2, PEER_SUB, tt // PEER_TOKEN_SPLIT), BF16),
                        pltpu.VMEM((PEER_HEADS, ch // PEER_KEYS, tt), F32),
                        pltpu.VMEM((PEER_HEADS, ch // PEER_KEYS, tt), F32)],
        compiler_params=pltpu.CompilerParams(dimension_semantics=("arbitrary",) * 3, vmem_limit_bytes=VMEM_LIMIT),
        name="peer",
    )(h2t, x_new, mod, g_post, wq_t, keys_b, u_b, vt_b)


def _rope_tables(L):
    nfreq = DH // 4
    inv = ROPE_BASE ** (-jnp.arange(nfreq, dtype=F32) / nfreq)
    t = jnp.arange(L, dtype=jnp.int32)
    rows = (t // GRID_W).astype(F32)[:, None] * inv[None, :]
    cols = (t % GRID_W).astype(F32)[:, None] * inv[None, :]
    cos64 = jnp.concatenate([jnp.cos(rows), jnp.cos(rows), jnp.cos(cols), jnp.cos(cols)], axis=1)
    sin64 = jnp.concatenate([-jnp.sin(rows), jnp.sin(rows), -jnp.sin(cols), jnp.sin(cols)], axis=1)
    return jnp.tile(cos64, (1, 2)), jnp.tile(sin64, (1, 2))


def _tile_plan(L):
    token_tile = 512
    query_tile = 256
    per_step = max(n for n in (1, 2, 4, 8) if L % (n * query_tile) == 0)
    plan = dict(token_tile=token_tile, query_tile=query_tile, query_tiles_per_step=per_step,
                peer_token_tile=512, peer_experts_per_step=2048)
    assert L % token_tile == 0 and L % plan["peer_token_tile"] == 0 and L % SGU_CHUNK == 0
    return plan


def kernel(x, c, ctx, c_ctx, w_ada, b_ada, g_pre_mix, g_post_mix, g_pre_ffn, g_post_ffn, w_in, b_gate, lambda_q1, lambda_k1, lambda_q2, lambda_k2, subln_g, sgu_ln_g, sgu_ln_b, sgu_w, sgu_b, w_branch_attn, w_branch_sgu, w_out, peer_w_query, peer_sub_keys, peer_u, peer_v):
    B, L, _ = x.shape
    depth = w_ada.shape[0]
    assert depth == 1 and B <= 7
    lam_init = 0.8 - 0.6 * math.exp(-0.3 * 0)
    tiles = _tile_plan(L)
    tm, tq = tiles["token_tile"], tiles["query_tile"]

    row = lambda a: a[0].reshape(1, -1)
    cc = jnp.concatenate([c, c_ctx[None, :], jnp.zeros((8 - B - 1, D), F32)], axis=0)
    mod, lam = _ada(cc, w_ada[0], row(b_ada), row(lambda_q1), row(lambda_k1), row(lambda_q2), row(lambda_k2), lam_init)

    win_b = w_in[0].astype(BF16)
    cos_t, sin_t = _rope_tables(L)
    sb_b = jnp.broadcast_to(sgu_b[0][:, :, None], (SGU_GROUPS, SGU_CHUNK, SGU_W // SGU_GROUPS))
    q, k, vt, t_sgu, g_attn = _inproj(x, mod, row(g_pre_mix), win_b, row(b_gate), cos_t, sin_t,
                                      row(sgu_ln_g), row(sgu_ln_b), sgu_w[0].astype(BF16), sb_b,
                                      w_branch_sgu[0].astype(BF16), tm)
    kc, vct = _ctxkv(ctx, mod, row(g_pre_mix), win_b, B)
    k_all = jnp.concatenate([kc, k], axis=1)
    vt_all = jnp.concatenate([vct, vt], axis=2)
    sg_t = jnp.broadcast_to(subln_g[0][:, None], (HEAD_W, tq))
    o = _attn(q, k_all, vt_all, lam, sg_t, lam_init, tq, tiles["query_tiles_per_step"])

    x_new, h2t = _mixout(o, g_attn, t_sgu, x, mod, row(g_post_mix), row(g_pre_ffn),
                         w_branch_attn[0].astype(BF16), w_out[0].astype(BF16), tm)
    return _peer(h2t, x_new, mod, row(g_post_ffn), peer_w_query[0].T.astype(BF16),
                 peer_sub_keys[0].astype(BF16), peer_u[0].astype(BF16), peer_v[0].T.astype(BF16),
                 tiles["peer_token_tile"], tiles["peer_experts_per_step"])
```
